```python
import math
import jax, jax.numpy as jnp
from jax import lax
import numpy as np

D_MODEL = 4096
BATCH = 2
SEQ = 8192
DEPTH = 2

CHUNK = 64
EPS = 1e-6
D_FF = 11008
FFN_RES = 0.5
N_BRANCH = 4

S5_W = D_MODEL // 4
S5_GROUP = 16
S5_GROUPS = S5_W // S5_GROUP
S5_STATE = 64
S5_DT_MIN = 1e-3
S5_DT_MAX = 1e-1

GLA_HEADS = 4
GLA_V = D_MODEL // 4
GLA_K = GLA_V // 2
GLA_DK = GLA_K // GLA_HEADS
GLA_DV = GLA_V // GLA_HEADS
GLA_RANK = 16
GLA_TAU = 16.0

CONV_W = D_MODEL // 4
CONV_TAPS = 3

LRU_W = D_MODEL // 4
LRU_BLOCKS = 16
LRU_BLOCK = LRU_W // LRU_BLOCKS
LRU_C = 8.0
LRU_CONV_TAPS = 4

IN_WIDTHS = (S5_W, GLA_K, GLA_K, GLA_V, GLA_RANK, GLA_V, CONV_W, CONV_W, CONV_W, LRU_W, LRU_W)
IN_TOTAL = sum(IN_WIDTHS)

kernel_name = "hybrid_parallel_s5_gla_conv_rglru_macaron"


def rmsnorm(x, g):
    xf = x.astype(jnp.float32)
    y = xf * lax.rsqrt(jnp.mean(xf * xf, axis=-1, keepdims=True) + EPS)
    return (y * g.astype(jnp.float32)).astype(x.dtype)


def swiglu(x, w_gate, w_up, w_down):
    return (jax.nn.silu(x @ w_gate) * (x @ w_up)) @ w_down


def split_cols(p, widths):
    outs, start = [], 0
    for w in widths:
        outs.append(p[..., start:start + w])
        start += w
    return outs


def causal_dwconv(x, w):
    k = w.shape[0]
    return lax.conv_general_dilated(
        x, w[:, None, :].astype(x.dtype), window_strides=(1,), padding=((k - 1, 0),),
        dimension_numbers=('NWC', 'WIO', 'NWC'), feature_group_count=x.shape[-1])


def _complex_op(e1, e2):
    a1r, a1i, b1r, b1i = e1
    a2r, a2i, b2r, b2i = e2
    return (a2r * a1r - a2i * a1i,
            a2r * a1i + a2i * a1r,
            a2r * b1r - a2i * b1i + b2r,
            a2r * b1i + a2i * b1r + b2i)


def _complex_diag_scan(a_re, a_im, b_re, b_im):
    a_re_t = jnp.broadcast_to(a_re, b_re.shape)
    a_im_t = jnp.broadcast_to(a_im, b_re.shape)
    _, _, h_re, h_im = lax.associative_scan(_complex_op, (a_re_t, a_im_t, b_re, b_im), axis=0)
    return h_re, h_im


def _linear_op(e1, e2):
    a1, b1 = e1
    a2, b2 = e2
    return a1 * a2, a2 * b1 + b2


def s5_mixer(u, lam_re, lam_im, log_dt, b_re, b_im, c_re, c_im, d_skip, w_glu, b_glu):
    bsz, seq, _ = u.shape
    f32 = jnp.float32
    dt = jnp.exp(log_dt.astype(f32))[:, None]
    lr, li = lam_re.astype(f32), lam_im.astype(f32)
    mag = jnp.exp(lr * dt)
    ab_re, ab_im = mag * jnp.cos(li * dt), mag * jnp.sin(li * dt)
    den = lr * lr + li * li
    nr, ni = ab_re - 1.0, ab_im
    z_re = (nr * lr + ni * li) / den
    z_im = (ni * lr - nr * li) / den
    br, bi = b_re.astype(f32), b_im.astype(f32)
    bb_re = z_re[..., None] * br - z_im[..., None] * bi
    bb_im = z_re[..., None] * bi + z_im[..., None] * br
    ug = u.astype(f32).reshape(bsz, seq, S5_GROUPS, S5_GROUP)
    bu_re = jnp.einsum('bsgh,gph->bsgp', ug, bb_re)
    bu_im = jnp.einsum('bsgh,gph->bsgp', ug, bb_im)
    h_re, h_im = jax.vmap(_complex_diag_scan, in_axes=(None, None, 0, 0))(ab_re, ab_im, bu_re, bu_im)
    y = (jnp.einsum('bsgp,ghp->bsgh', h_re, c_re.astype(f32))
         - jnp.einsum('bsgp,ghp->bsgh', h_im, c_im.astype(f32))
         + d_skip.astype(f32).reshape(S5_GROUPS, S5_GROUP) * ug)
    y = jax.nn.gelu(y.reshape(bsz, seq, S5_W))
    y = y * jax.nn.sigmoid(y @ w_glu.astype(f32) + b_glu.astype(f32))
    return y.astype(u.dtype)


def gla_mixer(q, k, v, g_lr, r, w_gate_up, b_gate, g_norm):
    bsz, seq, _ = q.shape
    n_chunks = seq // CHUNK
    f32 = jnp.float32
    q = q.astype(f32).reshape(bsz, n_chunks, CHUNK, GLA_HEADS, GLA_DK) * (GLA_DK ** -0.5)
    k = k.astype(f32).reshape(bsz, n_chunks, CHUNK, GLA_HEADS, GLA_DK)
    v = v.astype(f32).reshape(bsz, n_chunks, CHUNK, GLA_HEADS, GLA_DV)
    log_a = jax.nn.log_sigmoid(g_lr.astype(f32) @ w_gate_up.astype(f32) + b_gate.astype(f32)) / GLA_TAU
    log_a = log_a.reshape(bsz, n_chunks, CHUNK, GLA_HEADS, GLA_DK)
    bcum = jnp.cumsum(log_a, axis=2)
    b_last = bcum[:, :, -1]
    rel = bcum - bcum[:, :, CHUNK // 2 - 1:CHUNK // 2]
    e_pos, e_neg = jnp.exp(rel), jnp.exp(-rel)
    a_low = jnp.einsum('bnthk,bnshk->bnhts', q * e_pos, k * e_neg)
    a_up = jnp.einsum('bnthk,bnshk->bnhts', q * e_neg, k * e_pos)
    tri = jnp.tril(jnp.ones((CHUNK, CHUNK), dtype=bool))
    attn = jnp.where(tri, a_low, a_up)
    o = jnp.einsum('bnhts,bnshv->bnthv', attn, v)
    kv = jnp.einsum('bnshk,bnshv->bnhkv', k * jnp.exp(b_last[:, :, None] - bcum), v)
    decay = jnp.exp(b_last)

    def step(state, inp):
        d, kv_n = inp
        return d[..., None] * state + kv_n, state

    s0 = jnp.zeros((bsz, GLA_HEADS, GLA_DK, GLA_DV), f32)
    _, s_prev = lax.scan(step, s0, (jnp.moveaxis(decay, 1, 0), jnp.moveaxis(kv, 1, 0)))
    s_prev = jnp.moveaxis(s_prev, 0, 1)
    o = o + jnp.einsum('bnthk,bnhkv->bnthv', q * jnp.exp(bcum), s_prev)
    o = o * lax.rsqrt(jnp.mean(o * o, axis=-1, keepdims=True) + EPS) * g_norm.astype(f32)
    o = o.reshape(bsz, seq, GLA_V) * jax.nn.silu(r.astype(f32))
    return o.astype(r.dtype)


def shortconv_mixer(h, gate_b, gate_c, w_conv):
    return gate_b * causal_dwconv(gate_c * h, w_conv)


def rglru_mixer(xb, gate, w_conv, b_conv, w_a, b_a, w_x, b_x, lam):
    bsz, seq, _ = xb.shape
    f32 = jnp.float32
    xc = (causal_dwconv(xb, w_conv) + b_conv).astype(f32)
    xblk = xc.reshape(bsz, seq, LRU_BLOCKS, LRU_BLOCK)
    r_t = jax.nn.sigmoid(jnp.einsum('bsgi,gij->bsgj', xblk, w_a.astype(f32)).reshape(bsz, seq, LRU_W) + b_a.astype(f32))
    i_t = jax.nn.sigmoid(jnp.einsum('bsgi,gij->bsgj', xblk, w_x.astype(f32)).reshape(bsz, seq, LRU_W) + b_x.astype(f32))
    log_a = -LRU_C * r_t * jax.nn.softplus(-lam.astype(f32))
    a_t = jnp.exp(log_a)
    b_t = jnp.sqrt(-jnp.expm1(2.0 * log_a)) * (i_t * xc)
    _, h = lax.associative_scan(_linear_op, (a_t, b_t), axis=1)
    return (h * jax.nn.gelu(gate.astype(f32))).astype(xb.dtype)


def setup_inputs(seed: int = 0) -> dict:
    key = jax.random.key(seed)
    ks = iter(jax.random.split(key, 48))
    f32 = jnp.float32

    def nrm(shape, scale):
        return jax.random.normal(next(ks), shape, f32) * scale

    def gain(shape):
        return 1.0 + nrm(shape, 0.01)

    L, D, F = DEPTH, D_MODEL, D_FF
    u_lru = jax.random.uniform(next(ks), (L, LRU_W), f32, minval=0.9, maxval=0.999)
    s_lru = u_lru ** (1.0 / LRU_C)
    lam_lru = jnp.log(s_lru) - jnp.log1p(-s_lru)
    lam_im = math.pi * jnp.arange(S5_STATE, dtype=f32)
    return {
        "x": nrm((BATCH, SEQ, D), 1.0),
        "norm_ffn1": gain((L, D)),
        "ffn1_w_gate": nrm((L, D, F), D ** -0.5),
        "ffn1_w_up": nrm((L, D, F), D ** -0.5),
        "ffn1_w_down": nrm((L, F, D), F ** -0.5),
        "norm_mix": gain((L, D)),
        "w_in": nrm((L, D, IN_TOTAL), D ** -0.5),
        "s5_lam_re": -0.5 + nrm((L, S5_GROUPS, S5_STATE), 0.01),
        "s5_lam_im": lam_im + nrm((L, S5_GROUPS, S5_STATE), 0.01),
        "s5_log_dt": jax.random.uniform(next(ks), (L, S5_GROUPS), f32, minval=math.log(S5_DT_MIN), maxval=math.log(S5_DT_MAX)),
        "s5_b_re": nrm((L, S5_GROUPS, S5_STATE, S5_GROUP), S5_GROUP ** -0.5),
        "s5_b_im": nrm((L, S5_GROUPS, S5_STATE, S5_GROUP), S5_GROUP ** -0.5),
        "s5_c_re": nrm((L, S5_GROUPS, S5_GROUP, S5_STATE), S5_STATE ** -0.5),
        "s5_c_im": nrm((L, S5_GROUPS, S5_GROUP, S5_STATE), S5_STATE ** -0.5),
        "s5_d": nrm((L, S5_W), 1.0),
        "s5_w_glu": nrm((L, S5_W, S5_W), S5_W ** -0.5),
        "s5_b_glu": nrm((L, S5_W), 0.01),
        "gla_w_gate_up": nrm((L, GLA_RANK, GLA_K), GLA_RANK ** -0.5),
        "gla_b_gate": nrm((L, GLA_K), 0.01),
        "gla_norm": gain((L, GLA_DV)),
        "conv_w": nrm((L, CONV_TAPS, CONV_W), CONV_TAPS ** -0.5),
        "lru_conv_w": nrm((L, LRU_CONV_TAPS, LRU_W), LRU_CONV_TAPS ** -0.5),
        "lru_conv_b": nrm((L, LRU_W), 0.01),
        "lru_w_a": nrm((L, LRU_BLOCKS, LRU_BLOCK, LRU_BLOCK), LRU_BLOCK ** -0.5),
        "lru_b_a": nrm((L, LRU_W), 0.01),
        "lru_w_x": nrm((L, LRU_BLOCKS, LRU_BLOCK, LRU_BLOCK), LRU_BLOCK ** -0.5),
        "lru_b_x": nrm((L, LRU_W), 0.01),
        "lru_lam": lam_lru,
        "w_br_s5": nrm((L, S5_W, D), S5_W ** -0.5),
        "w_br_gla": nrm((L, GLA_V, D), GLA_V ** -0.5),
        "w_br_conv": nrm((L, CONV_W, D), CONV_W ** -0.5),
        "w_br_lru": nrm((L, LRU_W, D), LRU_W ** -0.5),
        "w_merge": nrm((L, N_BRANCH, D, D), D ** -0.5),
        "b_merge": nrm((L, N_BRANCH, D), 0.01),
        "w_out": nrm((L, D, D), D ** -0.5),
        "norm_ffn2": gain((L, D)),
        "ffn2_w_gate": nrm((L, D, F), D ** -0.5),
        "ffn2_w_up": nrm((L, D, F), D ** -0.5),
        "ffn2_w_down": nrm((L, F, D), F ** -0.5),
        "norm_final": gain((D,)),
    }


def reference(x, norm_ffn1, ffn1_w_gate, ffn1_w_up, ffn1_w_down, norm_mix, w_in,
              s5_lam_re, s5_lam_im, s5_log_dt, s5_b_re, s5_b_im, s5_c_re, s5_c_im, s5_d, s5_w_glu, s5_b_glu,
              gla_w_gate_up, gla_b_gate, gla_norm, conv_w,
              lru_conv_w, lru_conv_b, lru_w_a, lru_b_a, lru_w_x, lru_b_x, lru_lam,
              w_br_s5, w_br_gla, w_br_conv, w_br_lru, w_merge, b_merge, w_out,
              norm_ffn2, ffn2_w_gate, ffn2_w_up, ffn2_w_down, norm_final):
    for l in range(DEPTH):
        h = rmsnorm(x, norm_ffn1[l])
        x = x + FFN_RES * swiglu(h, ffn1_w_gate[l], ffn1_w_up[l], ffn1_w_down[l])

        xn = rmsnorm(x, norm_mix[l])
        proj = xn @ w_in[l]
        (s5_u, g_q, g_k, g_v, g_lr, g_r, c_h, c_b, c_c, r_x, r_g) = split_cols(proj, IN_WIDTHS)
        y_s5 = s5_mixer(s5_u, s5_lam_re[l], s5_lam_im[l], s5_log_dt[l], s5_b_re[l], s5_b_im[l],
                        s5_c_re[l], s5_c_im[l], s5_d[l], s5_w_glu[l], s5_b_glu[l])
        y_gla = gla_mixer(g_q, g_k, g_v, g_lr, g_r, gla_w_gate_up[l], gla_b_gate[l], gla_norm[l])
        y_conv = shortconv_mixer(c_h, c_b, c_c, conv_w[l])
        y_lru = rglru_mixer(r_x, r_g, lru_conv_w[l], lru_conv_b[l], lru_w_a[l], lru_b_a[l],
                            lru_w_x[l], lru_b_x[l], lru_lam[l])
        branches = ((y_s5, w_br_s5[l]), (y_gla, w_br_gla[l]), (y_conv, w_br_conv[l]), (y_lru, w_br_lru[l]))
        merged = None
        for n, (y_b, w_b) in enumerate(branches):
            gate = jax.nn.sigmoid(xn @ w_merge[l, n] + b_merge[l, n])
            term = gate * (y_b @ w_b)
            merged = term if merged is None else merged + term
        x = x + merged @ w_out[l]

        h = rmsnorm(x, norm_ffn2[l])
        x = x + FFN_RES * swiglu(h, ffn2_w_gate[l], ffn2_w_up[l], ffn2_w_down[l])
    return rmsnorm(x, norm_final)
```

```python
import functools
import math

import jax
import jax.numpy as jnp
from jax import lax
from jax.experimental import pallas as pl
from jax.experimental.pallas import tpu as pltpu

F32 = jnp.float32
BF16 = jnp.bfloat16

EPS = 1e-6
CHUNK = 64
FFN_RES = 0.5
SUBLANES = 8
MIB = 2 ** 20
EPI_ROWS = 16

S5_GROUP = 16
S5_STATE = 64
S5_COLS = 256
S5_GPC = S5_COLS // S5_GROUP
S5_LANES = S5_GPC * S5_STATE
GLA_HEADS = 4
GLA_RANK = 16
GLA_RANK_PAD = 128
GLA_TAU = 16.0
LRU_BLOCK = 64
LRU_C = 8.0
LRU_COLS = 256
MIX_W = 1024


def _cparams(sem, vmem_mib):
    return pltpu.CompilerParams(dimension_semantics=sem, vmem_limit_bytes=vmem_mib * MIB)


def _dot(a, b):
    return jnp.dot(a, b, preferred_element_type=F32)


def _dot_nt(a, b):
    return lax.dot_general(a, b, (((1,), (1,)), ((), ())), preferred_element_type=F32)


def _dot_tn(a, b):
    return lax.dot_general(a, b, (((0,), (0,)), ((), ())), preferred_element_type=F32)


def _softplus(x):
    return jnp.maximum(x, 0.0) + jnp.log1p(jnp.exp(-jnp.abs(x)))


def _gelu_tanh(x):
    return 0.5 * x * (1.0 + jnp.tanh(math.sqrt(2.0 / math.pi) * (x + 0.044715 * (x * x * x))))


def _rms_scale(x, g):
    ms = jnp.mean(x * x, axis=-1, keepdims=True)
    return x * lax.rsqrt(ms + EPS) * g


def _rmsnorm_kernel(x_ref, g_ref, o_ref):
    g = g_ref[...]

    def rows_body(r, carry):
        rows = pl.ds(pl.multiple_of(r * EPI_ROWS, EPI_ROWS), EPI_ROWS)
        o_ref[rows, :] = _rms_scale(x_ref[rows, :], g).astype(o_ref.dtype)
        return carry

    lax.fori_loop(0, x_ref.shape[0] // EPI_ROWS, rows_body, 0)


def _rmsnorm(x, g, out_dtype, tm=256):
    t, d = x.shape
    return pl.pallas_call(
        _rmsnorm_kernel,
        grid=(t // tm,),
        in_specs=[pl.BlockSpec((tm, d), lambda i: (i, 0)),
                  pl.BlockSpec((1, d), lambda i: (0, 0))],
        out_specs=pl.BlockSpec((tm, d), lambda i: (i, 0)),
        out_shape=jax.ShapeDtypeStruct((t, d), out_dtype),
        compiler_params=_cparams(("parallel",), 32),
        name="rmsnorm",
    )(x, g.reshape(1, d))


def _matmul_kernel(x_ref, w_ref, o_ref):
    o_ref[...] = _dot(x_ref[...], w_ref[...]).astype(o_ref.dtype)


def _matmul(x, w, out_dtype, tm, tn):
    t, k = x.shape
    n = w.shape[1]
    return pl.pallas_call(
        _matmul_kernel,
        grid=(t // tm, n // tn),
        in_specs=[pl.BlockSpec((tm, k), lambda i, j: (i, 0)),
                  pl.BlockSpec((k, tn), lambda i, j: (0, j))],
        out_specs=pl.BlockSpec((tm, tn), lambda i, j: (i, j)),
        out_shape=jax.ShapeDtypeStruct((t, n), out_dtype),
        compiler_params=_cparams(("parallel", "arbitrary"), 48),
        name="proj_matmul",
    )(x, w)


def _ffn_up_kernel(h_ref, wg_ref, wu_ref, o_ref):
    h = h_ref[...]
    g = _dot(h, wg_ref[...])
    u = _dot(h, wu_ref[...])
    o_ref[...] = (g * jax.nn.sigmoid(g) * u).astype(o_ref.dtype)


def _ffn_up(h, wg, wu, tm=1024, tn=512):
    t, d = h.shape
    f = wg.shape[1]
    return pl.pallas_call(
        _ffn_up_kernel,
        grid=(t // tm, f // tn),
        in_specs=[pl.BlockSpec((tm, d), lambda i, j: (i, 0)),
                  pl.BlockSpec((d, tn), lambda i, j: (0, j)),
                  pl.BlockSpec((d, tn), lambda i, j: (0, j))],
        out_specs=pl.BlockSpec((tm, tn), lambda i, j: (i, j)),
        out_shape=jax.ShapeDtypeStruct((t, f), BF16),
        compiler_params=_cparams(("parallel", "arbitrary"), 56),
        name="ffn_up",
    )(h, wg, wu)


def _mm_res_norm_kernel(a_ref, w_ref, xres_ref, g_ref, *refs, scale, nk, keep_x):
    if keep_x:
        acc_ref, hout_ref = refs
    else:
        hout_ref, acc_ref = refs
    k = pl.program_id(1)

    @pl.when(k == 0)
    def _():
        acc_ref[...] = jnp.zeros_like(acc_ref)

    acc_ref[...] += _dot(a_ref[...], w_ref[...])

    @pl.when(k == nk - 1)
    def _():
        g = g_ref[...]

        def rows_body(r, carry):
            rows = pl.ds(pl.multiple_of(r * EPI_ROWS, EPI_ROWS), EPI_ROWS)
            xo = xres_ref[rows, :] + scale * acc_ref[rows, :]
            if keep_x:
                acc_ref[rows, :] = xo
            hout_ref[rows, :] = _rms_scale(xo, g).astype(hout_ref.dtype)
            return carry

        lax.fori_loop(0, acc_ref.shape[0] // EPI_ROWS, rows_body, 0)


def _mm_res_norm(a, w, xres, g, scale, h_dtype, keep_x=True, tm=512, tk=512):
    t, kdim = a.shape
    d = w.shape[1]
    nk = kdim // tk
    row_blk = pl.BlockSpec((tm, d), lambda i, k: (i, 0))
    out_shape = [jax.ShapeDtypeStruct((t, d), h_dtype)]
    if keep_x:
        out_shape = [jax.ShapeDtypeStruct((t, d), F32)] + out_shape
    outs = pl.pallas_call(
        functools.partial(_mm_res_norm_kernel, scale=scale, nk=nk, keep_x=keep_x),
        grid=(t // tm, nk),
        in_specs=[pl.BlockSpec((tm, tk), lambda i, k: (i, k)),
                  pl.BlockSpec((tk, d), lambda i, k: (k, 0)),
                  row_blk,
                  pl.BlockSpec((1, d), lambda i, k: (0, 0))],
        out_specs=[row_blk] * len(out_shape),
        out_shape=out_shape,
        scratch_shapes=[] if keep_x else [pltpu.VMEM((tm, d), F32)],
        compiler_params=_cparams(("parallel", "arbitrary"), 60),
        name="mm_res_norm" if keep_x else "mm_res_norm_final",
    )(a, w, xres, g.reshape(1, d))
    return outs if keep_x else (None, outs[0])


def _merge_kernel(xn_ref, y0_ref, y1_ref, y2_ref, y3_ref, wm_ref, bm_ref, wb_ref, o_ref):
    xn = xn_ref[...]
    acc = None
    for n, y_ref in enumerate((y0_ref, y1_ref, y2_ref, y3_ref)):
        gate = jax.nn.sigmoid(_dot(xn, wm_ref[n]) + bm_ref[n])
        term = gate * _dot(y_ref[...], wb_ref[n])
        acc = term if acc is None else acc + term
    o_ref[...] = acc.astype(o_ref.dtype)


def _merge(xn, ys, wm, bm, wb, tm=512, tn=256):
    t, d = xn.shape
    nb, w = wb.shape[0], wb.shape[1]
    y_spec = pl.BlockSpec((tm, w), lambda j, i: (i, 0))
    return pl.pallas_call(
        _merge_kernel,
        grid=(d // tn, t // tm),
        in_specs=[pl.BlockSpec((tm, d), lambda j, i: (i, 0)),
                  y_spec, y_spec, y_spec, y_spec,
                  pl.BlockSpec((nb, d, tn), lambda j, i: (0, 0, j)),
                  pl.BlockSpec((nb, 1, tn), lambda j, i: (0, 0, j)),
                  pl.BlockSpec((nb, w, tn), lambda j, i: (0, 0, j))],
        out_specs=pl.BlockSpec((tm, tn), lambda j, i: (i, j)),
        out_shape=jax.ShapeDtypeStruct((t, d), BF16),
        compiler_params=_cparams(("parallel", "arbitrary"), 56),
        name="merge",
    )(xn, *ys, wm, bm, wb)


def _shift_rows(x, prev8, k):
    xs = pltpu.roll(x, k, 0)
    ps = pltpu.roll(prev8, k, 0)
    row = lax.broadcasted_iota(jnp.int32, prev8.shape, 0)
    head = jnp.where(row < k, ps, xs[:SUBLANES])
    return jnp.concatenate([head, xs[SUBLANES:]], axis=0)


def _conv_kernel(h_ref, b_ref, c_ref, w_ref, o_ref, carry_ref):
    @pl.when(pl.program_id(1) == 0)
    def _():
        carry_ref[...] = jnp.zeros_like(carry_ref)

    z = c_ref[...] * h_ref[...]
    prev = carry_ref[...]
    w = w_ref[...]
    y = w[0:1] * _shift_rows(z, prev, 2) + w[1:2] * _shift_rows(z, prev, 1) + w[2:3] * z
    o_ref[...] = (b_ref[...] * y).astype(o_ref.dtype)
    carry_ref[...] = z[z.shape[0] - SUBLANES:]


def _conv_mixer(proj, w, bsz, seq, cols, ts=512):
    nt = seq // ts
    col = lambda c: pl.BlockSpec((ts, MIX_W), lambda b, t, c=c: (b * nt + t, c))
    return pl.pallas_call(
        _conv_kernel,
        grid=(bsz, nt),
        in_specs=[col(cols[0]), col(cols[1]), col(cols[2]),
                  pl.BlockSpec(w.shape, lambda b, t: (0, 0))],
        out_specs=pl.BlockSpec((ts, MIX_W), lambda b, t: (b * nt + t, 0)),
        out_shape=jax.ShapeDtypeStruct((bsz * seq, MIX_W), BF16),
        scratch_shapes=[pltpu.VMEM((SUBLANES, MIX_W), F32)],
        compiler_params=_cparams(("parallel", "arbitrary"), 32),
        name="conv_mixer",
    )(proj, proj, proj, w)


def _lru_kernel(x_ref, gate_ref, cw_ref, cb_ref, wa_ref, ba_ref, wx_ref, bx_ref, lam_ref, o_ref,
                xprev_ref, hprev_ref, a_s, b_s):
    ts = x_ref.shape[0]

    @pl.when(pl.program_id(1) == 0)
    def _():
        xprev_ref[...] = jnp.zeros_like(xprev_ref)
        hprev_ref[...] = jnp.zeros_like(hprev_ref)

    xb = x_ref[...]
    prev = xprev_ref[...]
    cw = cw_ref[...]
    xc = (cw[0:1] * _shift_rows(xb, prev, 3) + cw[1:2] * _shift_rows(xb, prev, 2)
          + cw[2:3] * _shift_rows(xb, prev, 1) + cw[3:4] * xb) + cb_ref[...]
    xprev_ref[...] = xb[ts - SUBLANES:]

    xcb = xc.astype(BF16)
    ra, rx = [], []
    for c in range(MIX_W // LRU_COLS):
        blk = xcb[:, c * LRU_COLS:(c + 1) * LRU_COLS]
        ra.append(_dot(blk, wa_ref[c]))
        rx.append(_dot(blk, wx_ref[c]))
    r_t = jax.nn.sigmoid(jnp.concatenate(ra, axis=1) + ba_ref[...])
    i_t = jax.nn.sigmoid(jnp.concatenate(rx, axis=1) + bx_ref[...])
    log_a = (-LRU_C) * r_t * _softplus(-lam_ref[...])
    th = jnp.tanh(log_a)
    a_s[...] = jnp.exp(log_a)
    b_s[...] = jnp.sqrt(-2.0 * th / (1.0 - th)) * (i_t * xc)

    row = lax.broadcasted_iota(jnp.int32, (SUBLANES, MIX_W), 0)

    def body(g, hp):
        r0 = pl.multiple_of(g * SUBLANES, SUBLANES)
        a = a_s[pl.ds(r0, SUBLANES), :]
        b = b_s[pl.ds(r0, SUBLANES), :]
        for k in (1, 2, 4):
            a_sh = jnp.where(row >= k, pltpu.roll(a, k, 0), 1.0)
            b_sh = jnp.where(row >= k, pltpu.roll(b, k, 0), 0.0)
            b = a * b_sh + b
            a = a * a_sh
        h = a * hp + b
        b_s[pl.ds(r0, SUBLANES), :] = h
        return jnp.broadcast_to(h[SUBLANES - 1:SUBLANES], (SUBLANES, MIX_W))

    hprev_ref[...] = lax.fori_loop(0, ts // SUBLANES, body, hprev_ref[...])
    o_ref[...] = (b_s[...] * _gelu_tanh(gate_ref[...])).astype(o_ref.dtype)


def _lru_mixer(proj, p, bsz, seq, cols, ts=256):
    nt = seq // ts
    col = lambda c: pl.BlockSpec((ts, MIX_W), lambda b, t, c=c: (b * nt + t, c))
    full = lambda a: pl.BlockSpec(a.shape, lambda b, t, n=a.ndim: (0,) * n)
    params = (p["lru_cw"], p["lru_cb"], p["lru_wa"], p["lru_ba"], p["lru_wx"], p["lru_bx"], p["lru_lam"])
    return pl.pallas_call(
        _lru_kernel,
        grid=(bsz, nt),
        in_specs=[col(cols[0]), col(cols[1])] + [full(a) for a in params],
        out_specs=pl.BlockSpec((ts, MIX_W), lambda b, t: (b * nt + t, 0)),
        out_shape=jax.ShapeDtypeStruct((bsz * seq, MIX_W), BF16),
        scratch_shapes=[pltpu.VMEM((SUBLANES, MIX_W), F32), pltpu.VMEM((SUBLANES, MIX_W), F32),
                        pltpu.VMEM((ts, MIX_W), F32), pltpu.VMEM((ts, MIX_W), F32)],
        compiler_params=_cparams(("parallel", "arbitrary"), 32),
        name="lru_mixer",
    )(proj, proj, *params)


def _s5_kernel(u_ref, bw_ref, cw_ref, pw_ref, d_ref, wglu_ref, bglu_ref, o_ref, s_re, s_im, carry_ref):
    ts = u_ref.shape[0]
    nblk = MIX_W // S5_COLS

    @pl.when(pl.program_id(1) == 0)
    def _():
        carry_ref[...] = jnp.zeros_like(carry_ref)

    u = u_ref[...]
    ub = u.astype(BF16)
    ys = []
    for c in range(nblk):
        bu = _dot(ub[:, c * S5_COLS:(c + 1) * S5_COLS], bw_ref[c])
        s_re[...] = bu[:, :S5_LANES]
        s_im[...] = bu[:, S5_LANES:]

        def body(g, hp, c=c):
            hpr, hpi = hp
            r0 = pl.multiple_of(g * SUBLANES, SUBLANES)
            xr = s_re[pl.ds(r0, SUBLANES), :]
            xi = s_im[pl.ds(r0, SUBLANES), :]
            for lvl, k in enumerate((1, 2, 4)):
                mr = pw_ref[c, 2 * lvl]
                mi = pw_ref[c, 2 * lvl + 1]
                sr = pltpu.roll(xr, k, 0)
                si = pltpu.roll(xi, k, 0)
                xr, xi = xr + (mr * sr - mi * si), xi + (mr * si + mi * sr)
            pr = pw_ref[c, 6]
            pi = pw_ref[c, 7]
            hr = xr + (pr * hpr - pi * hpi)
            hi = xi + (pr * hpi + pi * hpr)
            s_re[pl.ds(r0, SUBLANES), :] = hr
            s_im[pl.ds(r0, SUBLANES), :] = hi
            return (jnp.broadcast_to(hr[SUBLANES - 1:SUBLANES], hr.shape),
                    jnp.broadcast_to(hi[SUBLANES - 1:SUBLANES], hi.shape))

        hpr, hpi = lax.fori_loop(0, ts // SUBLANES, body, (carry_ref[c, 0], carry_ref[c, 1]))
        carry_ref[c, 0] = hpr
        carry_ref[c, 1] = hpi
        hb = jnp.concatenate([s_re[...], s_im[...]], axis=1).astype(BF16)
        ys.append(_dot(hb, cw_ref[c]))
    y = jnp.concatenate(ys, axis=1) + d_ref[...] * u
    y = _gelu_tanh(y)
    z = _dot(y.astype(BF16), wglu_ref[...]) + bglu_ref[...]
    o_ref[...] = (y * jax.nn.sigmoid(z)).astype(o_ref.dtype)


def _s5_mixer(proj, p, bsz, seq, col_idx, ts=256):
    nt = seq // ts
    nblk = MIX_W // S5_COLS
    full = lambda a: pl.BlockSpec(a.shape, lambda b, t, n=a.ndim: (0,) * n)
    params = (p["s5_bw"], p["s5_cw"], p["s5_pw"], p["s5_d"], p["s5_wglu"], p["s5_bglu"])
    return pl.pallas_call(
        _s5_kernel,
        grid=(bsz, nt),
        in_specs=[pl.BlockSpec((ts, MIX_W), lambda b, t: (b * nt + t, col_idx))] + [full(a) for a in params],
        out_specs=pl.BlockSpec((ts, MIX_W), lambda b, t: (b * nt + t, 0)),
        out_shape=jax.ShapeDtypeStruct((bsz * seq, MIX_W), BF16),
        scratch_shapes=[pltpu.VMEM((ts, S5_LANES), F32), pltpu.VMEM((ts, S5_LANES), F32),
                        pltpu.VMEM((nblk, 2, SUBLANES, S5_LANES), F32)],
        compiler_params=_cparams(("parallel", "arbitrary"), 48),
        name="s5_mixer",
    )(proj, *params)


def _gla_kernel(q_ref, k_ref, v_ref, r_ref, glr_ref, wgu_ref, bg_ref, gn_ref, o_ref, state_ref):
    ts = q_ref.shape[0]
    dk = q_ref.shape[1] // GLA_HEADS
    dv = v_ref.shape[1] // GLA_HEADS
    qscale = dk ** -0.5

    @pl.when(pl.program_id(1) == 0)
    def _():
        state_ref[...] = jnp.zeros_like(state_ref)

    ri = lax.broadcasted_iota(jnp.int32, (CHUNK, CHUNK), 0)
    ci = lax.broadcasted_iota(jnp.int32, (CHUNK, CHUNK), 1)
    tri = ri >= ci
    tri_ones = jnp.where(tri, 1.0, 0.0).astype(F32)
    gn = gn_ref[...]

    for n in range(ts // CHUNK):
        rows = slice(n * CHUNK, (n + 1) * CHUNK)
        x = _dot(glr_ref[rows, :].astype(BF16), wgu_ref[...]) + bg_ref[...]
        log_a = -_softplus(-x) / GLA_TAU
        bcum = jnp.dot(tri_ones, log_a, preferred_element_type=F32, precision=lax.Precision.HIGHEST)
        b_last = bcum[CHUNK - 1:CHUNK]
        rel = bcum - bcum[CHUNK // 2 - 1:CHUNK // 2]
        e_pos = jnp.exp(rel)
        e_neg = jnp.exp(-rel)
        e_cum = jnp.exp(bcum)
        e_last = jnp.exp(b_last - bcum)
        decay = jnp.exp(b_last)
        for h in range(GLA_HEADS):
            ks = slice(h * dk, (h + 1) * dk)
            vs = slice(h * dv, (h + 1) * dv)
            qh = q_ref[rows, ks] * qscale
            kh = k_ref[rows, ks]
            vh = v_ref[rows, vs].astype(BF16)
            a_low = _dot_nt((qh * e_pos[:, ks]).astype(BF16), (kh * e_neg[:, ks]).astype(BF16))
            a_up = _dot_nt((qh * e_neg[:, ks]).astype(BF16), (kh * e_pos[:, ks]).astype(BF16))
            attn = jnp.where(tri, a_low, a_up)
            o = _dot(attn.astype(BF16), vh)
            st = state_ref[h]
            o = o + _dot_nt((qh * e_cum[:, ks]).astype(BF16), st.astype(BF16))
            kv_t = _dot_tn(vh, (kh * e_last[:, ks]).astype(BF16))
            state_ref[h] = decay[:, ks] * st + kv_t
            o = o * lax.rsqrt(jnp.mean(o * o, axis=-1, keepdims=True) + EPS) * gn
            rg = r_ref[rows, vs]
            o_ref[rows, vs] = (o * (rg * jax.nn.sigmoid(rg))).astype(o_ref.dtype)


def _gla_mixer(proj, glr, p, bsz, seq, cols, ts=256):
    nt = seq // ts
    kw = MIX_W // 2
    full = lambda a: pl.BlockSpec(a.shape, lambda b, t, n=a.ndim: (0,) * n)
    params = (p["gla_wgu"], p["gla_bg"], p["gla_norm"])
    dv = MIX_W // GLA_HEADS
    dk = kw // GLA_HEADS
    return pl.pallas_call(
        _gla_kernel,
        grid=(bsz, nt),
        in_specs=[pl.BlockSpec((ts, kw), lambda b, t: (b * nt + t, cols[0])),
                  pl.BlockSpec((ts, kw), lambda b, t: (b * nt + t, cols[1])),
                  pl.BlockSpec((ts, MIX_W), lambda b, t: (b * nt + t, cols[2])),
                  pl.BlockSpec((ts, MIX_W), lambda b, t: (b * nt + t, cols[3])),
                  pl.BlockSpec((ts, GLA_RANK_PAD), lambda b, t: (b * nt + t, 0))]
                 + [full(a) for a in params],
        out_specs=pl.BlockSpec((ts, MIX_W), lambda b, t: (b * nt + t, 0)),
        out_shape=jax.ShapeDtypeStruct((bsz * seq, MIX_W), BF16),
        scratch_shapes=[pltpu.VMEM((GLA_HEADS, dv, dk), F32)],
        compiler_params=_cparams(("parallel", "arbitrary"), 32),
        name="gla_mixer",
    )(proj, proj, proj, proj, glr, *params)


def _block_diag(blocks):
    n, g, r, c = blocks.shape
    eye = jnp.eye(g, dtype=blocks.dtype)
    return jnp.einsum('ngrc,gk->ngrkc', blocks, eye).reshape(n, g * r, g * c)


def _s5_params(lam_re, lam_im, log_dt, b_re, b_im, c_re, c_im):
    groups = lam_re.shape[0]
    nblk = groups // S5_GPC
    dt = jnp.exp(log_dt)[:, None]
    mag = jnp.exp(lam_re * dt)
    ab_re, ab_im = mag * jnp.cos(lam_im * dt), mag * jnp.sin(lam_im * dt)
    den = lam_re * lam_re + lam_im * lam_im
    nr, ni = ab_re - 1.0, ab_im
    z_re = (nr * lam_re + ni * lam_im) / den
    z_im = (ni * lam_re - nr * lam_im) / den
    bb_re = z_re[..., None] * b_re - z_im[..., None] * b_im
    bb_im = z_re[..., None] * b_im + z_im[..., None] * b_re
    to_blk = lambda a: _block_diag(jnp.swapaxes(a, 1, 2).reshape(nblk, S5_GPC, S5_GROUP, S5_STATE))
    bw = jnp.concatenate([to_blk(bb_re), to_blk(bb_im)], axis=2).astype(BF16)
    from_blk = lambda a: _block_diag(jnp.swapaxes(a, 1, 2).reshape(nblk, S5_GPC, S5_STATE, S5_GROUP))
    cw = jnp.concatenate([from_blk(c_re), from_blk(-c_im)], axis=1).astype(BF16)
    ar = ab_re.reshape(nblk, S5_LANES)
    ai = ab_im.reshape(nblk, S5_LANES)
    pows = [(ar, ai)]
    for _ in range(SUBLANES - 1):
        pr, pi = pows[-1]
        pows.append((pr * ar - pi * ai, pr * ai + pi * ar))
    row = jnp.arange(SUBLANES)[None, :, None]
    planes = []
    for k in (1, 2, 4):
        for comp in (0, 1):
            planes.append(jnp.where(row >= k, pows[k - 1][comp][:, None, :], 0.0))
    planes.append(jnp.stack([pw[0] for pw in pows], axis=1))
    planes.append(jnp.stack([pw[1] for pw in pows], axis=1))
    pw = jnp.stack(planes, axis=1).astype(F32)
    return bw, cw, pw


def _layer_params(l, a):
    d = a["w_in"].shape[1]
    f = a["ffn1_w_gate"].shape[2]
    f_pad = -(-f // 512) * 512
    bf_pad_cols = lambda w: jnp.pad(w.astype(BF16), ((0, 0), (0, f_pad - f)))
    bf_pad_rows = lambda w: jnp.pad(w.astype(BF16), ((0, f_pad - f), (0, 0)))
    row = lambda v: v.reshape(1, -1).astype(F32)
    p = {}
    for tag in ("ffn1", "ffn2"):
        p[tag + "_wg"] = bf_pad_cols(a[tag + "_w_gate"][l])
        p[tag + "_wu"] = bf_pad_cols(a[tag + "_w_up"][l])
        p[tag + "_wd"] = bf_pad_rows(a[tag + "_w_down"][l])
    w_in = a["w_in"][l]
    glr0 = MIX_W + MIX_W // 2 + MIX_W // 2 + MIX_W
    p["w_main"] = jnp.concatenate([w_in[:, :glr0], w_in[:, glr0 + GLA_RANK:]], axis=1).astype(BF16)
    p["w_glr"] = jnp.pad(w_in[:, glr0:glr0 + GLA_RANK].astype(BF16), ((0, 0), (0, GLA_RANK_PAD - GLA_RANK)))
    p["s5_bw"], p["s5_cw"], p["s5_pw"] = _s5_params(
        a["s5_lam_re"][l], a["s5_lam_im"][l], a["s5_log_dt"][l], a["s5_b_re"][l], a["s5_b_im"][l],
        a["s5_c_re"][l], a["s5_c_im"][l])
    p["s5_d"] = row(a["s5_d"][l])
    p["s5_wglu"] = a["s5_w_glu"][l].astype(BF16)
    p["s5_bglu"] = row(a["s5_b_glu"][l])
    p["gla_wgu"] = jnp.pad(a["gla_w_gate_up"][l].astype(BF16), ((0, GLA_RANK_PAD - GLA_RANK), (0, 0)))
    p["gla_bg"] = row(a["gla_b_gate"][l])
    p["gla_norm"] = row(a["gla_norm"][l])
    p["conv_w"] = a["conv_w"][l].astype(F32)
    p["lru_cw"] = a["lru_conv_w"][l].astype(F32)
    p["lru_cb"] = row(a["lru_conv_b"][l])
    per_blk = LRU_COLS // LRU_BLOCK
    bd = lambda w: _block_diag(w.reshape(-1, per_blk, LRU_BLOCK, LRU_BLOCK)).astype(BF16)
    p["lru_wa"] = bd(a["lru_w_a"][l])
    p["lru_wx"] = bd(a["lru_w_x"][l])
    p["lru_ba"] = row(a["lru_b_a"][l])
    p["lru_bx"] = row(a["lru_b_x"][l])
    p["lru_lam"] = row(a["lru_lam"][l])
    p["w_br"] = jnp.stack([a["w_br_s5"][l], a["w_br_gla"][l], a["w_br_conv"][l], a["w_br_lru"][l]]).astype(BF16)
    p["w_merge"] = a["w_merge"][l].astype(BF16)
    p["b_merge"] = a["b_merge"][l].reshape(-1, 1, d).astype(F32)
    p["w_out"] = a["w_out"][l].astype(BF16)
    return p


_COL_S5 = 0
_COLS_GLA = (2, 3, 2, 3)
_COLS_CONV = (4, 5, 6)
_COLS_LRU = (7, 8)


def kernel(x, norm_ffn1, ffn1_w_gate, ffn1_w_up, ffn1_w_down, norm_mix, w_in, s5_lam_re, s5_lam_im, s5_log_dt, s5_b_re, s5_b_im, s5_c_re, s5_c_im, s5_d, s5_w_glu, s5_b_glu, gla_w_gate_up, gla_b_gate, gla_norm, conv_w, lru_conv_w, lru_conv_b, lru_w_a, lru_b_a, lru_w_x, lru_b_x, lru_lam, w_br_s5, w_br_gla, w_br_conv, w_br_lru, w_merge, b_merge, w_out, norm_ffn2, ffn2_w_gate, ffn2_w_up, ffn2_w_down, norm_final):
    a = dict(locals())
    bsz, seq, d = x.shape
    depth = norm_ffn1.shape[0]
    xf = x.reshape(bsz * seq, d)
    h = _rmsnorm(xf, norm_ffn1[0], BF16)
    for l in range(depth):
        p = _layer_params(l, a)
        act = _ffn_up(h, p["ffn1_wg"], p["ffn1_wu"])
        xf, xn = _mm_res_norm(act, p["ffn1_wd"], xf, norm_mix[l], FFN_RES, BF16)

        proj = _matmul(xn, p["w_main"], F32, tm=1024, tn=512)
        glr = _matmul(xn, p["w_glr"], F32, tm=1024, tn=GLA_RANK_PAD)
        y_s5 = _s5_mixer(proj, p, bsz, seq, _COL_S5)
        y_gla = _gla_mixer(proj, glr, p, bsz, seq, _COLS_GLA)
        y_conv = _conv_mixer(proj, p["conv_w"], bsz, seq, _COLS_CONV)
        y_lru = _lru_mixer(proj, p, bsz, seq, _COLS_LRU)
        merged = _merge(xn, (y_s5, y_gla, y_conv, y_lru), p["w_merge"], p["b_merge"], p["w_br"])
        xf, h = _mm_res_norm(merged, p["w_out"], xf, norm_ffn2[l], 1.0, BF16)

        act = _ffn_up(h, p["ffn2_wg"], p["ffn2_wu"])
        last = l == depth - 1
        g_next = norm_final if last else norm_ffn1[l + 1]
        xf, h = _mm_res_norm(act, p["ffn2_wd"], xf, g_next, FFN_RES, F32 if last else BF16, keep_x=not last)
    return h.reshape(bsz, seq, d)
```

```python
import functools
import math

import jax
import jax.numpy as jnp
from jax import lax
from jax.experimental import pallas as pl
from jax.experimental.pallas import tpu as pltpu

F32 = jnp.float32
BF16 = jnp.bfloat16

EPS = 1e-6
CHUNK = 64
FFN_RES = 0.5
SUBLANES = 8
MIB = 2 ** 20
EPI_ROWS = 16
CAST_ROWS = 256
WD_SLAB = 16

S5_GROUP = 16
S5_STATE = 64
S5_COLS = 256
S5_GPC = S5_COLS // S5_GROUP
S5_LANES = S5_GPC * S5_STATE
GLA_HEADS = 4
GLA_RANK = 16
GLA_RANK_PAD = 128
GLA_TAU = 16.0
LRU_BLOCK = 64
LRU_C = 8.0
LRU_COLS = 256
MIX_W = 1024


def _cparams(sem, vmem_mib):
    return pltpu.CompilerParams(dimension_semantics=sem, vmem_limit_bytes=vmem_mib * MIB)


def _dot(a, b):
    return jnp.dot(a, b, preferred_element_type=F32)


def _dot_nt(a, b):
    return lax.dot_general(a, b, (((1,), (1,)), ((), ())), preferred_element_type=F32)


def _dot_tn(a, b):
    return lax.dot_general(a, b, (((0,), (0,)), ((), ())), preferred_element_type=F32)


def _softplus(x):
    return jnp.maximum(x, 0.0) + jnp.log1p(jnp.exp(-jnp.abs(x)))


def _gelu_tanh(x):
    return 0.5 * x * (1.0 + jnp.tanh(math.sqrt(2.0 / math.pi) * (x + 0.044715 * (x * x * x))))


def _rms_scale(x, g):
    ms = jnp.mean(x * x, axis=-1, keepdims=True)
    return x * lax.rsqrt(ms + EPS) * g


def _rmsnorm_kernel(x_ref, g_ref, o_ref):
    g = g_ref[...]

    def rows_body(r, carry):
        rows = pl.ds(pl.multiple_of(r * EPI_ROWS, EPI_ROWS), EPI_ROWS)
        o_ref[rows, :] = _rms_scale(x_ref[rows, :], g).astype(o_ref.dtype)
        return carry

    lax.fori_loop(0, x_ref.shape[0] // EPI_ROWS, rows_body, 0)


def _rmsnorm(x, g, out_dtype, tm=512):
    t, d = x.shape
    return pl.pallas_call(
        _rmsnorm_kernel,
        grid=(t // tm,),
        in_specs=[pl.BlockSpec((tm, d), lambda i: (i, 0)),
                  pl.BlockSpec((1, d), lambda i: (0, 0))],
        out_specs=pl.BlockSpec((tm, d), lambda i: (i, 0)),
        out_shape=jax.ShapeDtypeStruct((t, d), out_dtype),
        compiler_params=_cparams(("parallel",), 40),
        name="rmsnorm",
    )(x, g.reshape(1, d))


def _matmul_kernel(x_ref, w_ref, o_ref):
    o_ref[...] = _dot(x_ref[...], w_ref[...]).astype(o_ref.dtype)


def _matmul(x, w, l, out_dtype, tm, tn):
    t, k = x.shape
    n = w.shape[2]
    return pl.pallas_call(
        _matmul_kernel,
        grid=(t // tm, n // tn),
        in_specs=[pl.BlockSpec((tm, k), lambda i, j: (i, 0)),
                  pl.BlockSpec((None, k, tn), lambda i, j: (l, 0, j))],
        out_specs=pl.BlockSpec((tm, tn), lambda i, j: (i, j)),
        out_shape=jax.ShapeDtypeStruct((t, n), out_dtype),
        compiler_params=_cparams(("parallel", "arbitrary"), 48),
        name="proj_matmul",
    )(x, w)


def _ffn_up_kernel(h_ref, wg_ref, wu_ref, wd_ref, o_ref, wdb_ref, wg_bf, wu_bf):
    @pl.when(pl.program_id(1) == 0)
    def _():
        def cast_rows(r, carry):
            rows = pl.ds(pl.multiple_of(r * CAST_ROWS, CAST_ROWS), CAST_ROWS)
            wg_bf[rows, :] = wg_ref[rows, :].astype(BF16)
            wu_bf[rows, :] = wu_ref[rows, :].astype(BF16)
            return carry

        lax.fori_loop(0, wg_ref.shape[0] // CAST_ROWS, cast_rows, 0)

    wdb_ref[...] = wd_ref[...].astype(BF16)
    h = h_ref[...]
    g = _dot(h, wg_bf[...])
    u = _dot(h, wu_bf[...])
    o_ref[...] = (g * jax.nn.sigmoid(g) * u).astype(o_ref.dtype)


def _ffn_up(h, wg, wu, wd, l, tm=512, tn=512):
    t, d = h.shape
    f = wg.shape[2]
    nj, ni = pl.cdiv(f, tn), t // tm
    slab_rows = next(r for r in range(WD_SLAB, f + 1, WD_SLAB) if f % r == 0 and f // r <= nj * ni)
    slabs = f // slab_rows
    slab = lambda j, i: jnp.minimum(j * ni + i, slabs - 1)
    return pl.pallas_call(
        _ffn_up_kernel,
        grid=(nj, ni),
        in_specs=[pl.BlockSpec((tm, d), lambda j, i: (i, 0)),
                  pl.BlockSpec((None, d, tn), lambda j, i: (l, 0, j)),
                  pl.BlockSpec((None, d, tn), lambda j, i: (l, 0, j)),
                  pl.BlockSpec((None, slab_rows, d), lambda j, i: (l, slab(j, i), 0))],
        out_specs=[pl.BlockSpec((tm, tn), lambda j, i: (i, j)),
                   pl.BlockSpec((slab_rows, d), lambda j, i: (slab(j, i), 0))],
        out_shape=[jax.ShapeDtypeStruct((t, f), BF16),
                   jax.ShapeDtypeStruct((f, d), BF16)],
        scratch_shapes=[pltpu.VMEM((d, tn), BF16), pltpu.VMEM((d, tn), BF16)],
        compiler_params=_cparams(("arbitrary", "arbitrary"), 60),
        name="ffn_up",
    )(h, wg, wu, wd)


def _mm_res_kernel(a_ref, w_ref, xres_ref, o_ref, acc_ref, *, scale, nk, k_last):
    k = pl.program_id(2)
    tk = a_ref.shape[1]

    @pl.when(k == 0)
    def _():
        acc_ref[...] = _dot(a_ref[...], w_ref[...])

    @pl.when((k > 0) & (k < nk - 1))
    def _():
        acc_ref[...] += _dot(a_ref[...], w_ref[...])

    @pl.when(k == nk - 1)
    def _():
        if k_last == tk:
            part = _dot(a_ref[...], w_ref[...])
        else:
            part = _dot(a_ref[:, :k_last], w_ref[:k_last, :])
        o_ref[...] = xres_ref[...] + scale * (acc_ref[...] + part)


def _mm_res(a, w, xres, scale, l=None, tm=1024, tn=2048, tk=1024):
    t, kdim = a.shape
    n = w.shape[-1]
    nk = pl.cdiv(kdim, tk)
    assert nk >= 2
    k_last = kdim - (nk - 1) * tk
    if l is None:
        w_spec = pl.BlockSpec((tk, tn), lambda i, j, k: (k, j))
    else:
        w_spec = pl.BlockSpec((None, tk, tn), lambda i, j, k: (l, k, j))
    return pl.pallas_call(
        functools.partial(_mm_res_kernel, scale=scale, nk=nk, k_last=k_last),
        grid=(t // tm, n // tn, nk),
        in_specs=[pl.BlockSpec((tm, tk), lambda i, j, k: (i, k)),
                  w_spec,
                  pl.BlockSpec((tm, tn), lambda i, j, k: (i, j))],
        out_specs=pl.BlockSpec((tm, tn), lambda i, j, k: (i, j)),
        out_shape=jax.ShapeDtypeStruct((t, n), F32),
        scratch_shapes=[pltpu.VMEM((tm, tn), F32)],
        compiler_params=_cparams(("parallel", "parallel", "arbitrary"), 60),
        name="mm_res",
    )(a, w, xres)


def _merge_kernel(xn_ref, y0_ref, y1_ref, y2_ref, y3_ref, wm_ref, bm_ref, w0_ref, w1_ref, w2_ref, w3_ref, o_ref):
    xn = xn_ref[...]
    acc = None
    branches = ((y0_ref, w0_ref), (y1_ref, w1_ref), (y2_ref, w2_ref), (y3_ref, w3_ref))
    for n, (y_ref, wb_ref) in enumerate(branches):
        gate = jax.nn.sigmoid(_dot(xn, wm_ref[n]) + bm_ref[n])
        term = gate * _dot(y_ref[...], wb_ref[...])
        acc = term if acc is None else acc + term
    o_ref[...] = acc.astype(o_ref.dtype)


def _merge(xn, ys, wm, bm, wbs, l, tm=512, tn=256):
    t, d = xn.shape
    nb = wm.shape[1]
    w = wbs[0].shape[1]
    y_spec = pl.BlockSpec((tm, w), lambda j, i: (i, 0))
    wb_spec = pl.BlockSpec((None, w, tn), lambda j, i: (l, 0, j))
    return pl.pallas_call(
        _merge_kernel,
        grid=(d // tn, t // tm),
        in_specs=[pl.BlockSpec((tm, d), lambda j, i: (i, 0)),
                  y_spec, y_spec, y_spec, y_spec,
                  pl.BlockSpec((None, nb, d, tn), lambda j, i: (l, 0, 0, j)),
                  pl.BlockSpec((None, nb, 1, tn), lambda j, i: (l, 0, 0, j)),
                  wb_spec, wb_spec, wb_spec, wb_spec],
        out_specs=pl.BlockSpec((tm, tn), lambda j, i: (i, j)),
        out_shape=jax.ShapeDtypeStruct((t, d), BF16),
        compiler_params=_cparams(("parallel", "arbitrary"), 56),
        name="merge",
    )(xn, *ys, wm, bm, *wbs)


def _shift_rows(x, prev8, k):
    xs = pltpu.roll(x, k, 0)
    ps = pltpu.roll(prev8, k, 0)
    row = lax.broadcasted_iota(jnp.int32, prev8.shape, 0)
    head = jnp.where(row < k, ps, xs[:SUBLANES])
    return jnp.concatenate([head, xs[SUBLANES:]], axis=0)


def _conv_kernel(h_ref, b_ref, c_ref, w_ref, o_ref, carry_ref):
    @pl.when(pl.program_id(1) == 0)
    def _():
        carry_ref[...] = jnp.zeros_like(carry_ref)

    z = c_ref[...] * h_ref[...]
    prev = carry_ref[...]
    w = w_ref[...]
    y = w[0:1] * _shift_rows(z, prev, 2) + w[1:2] * _shift_rows(z, prev, 1) + w[2:3] * z
    o_ref[...] = (b_ref[...] * y).astype(o_ref.dtype)
    carry_ref[...] = z[z.shape[0] - SUBLANES:]


def _conv_mixer(proj, w, bsz, seq, cols, ts=512):
    nt = seq // ts
    col = lambda c: pl.BlockSpec((ts, MIX_W), lambda b, t, c=c: (b * nt + t, c))
    return pl.pallas_call(
        _conv_kernel,
        grid=(bsz, nt),
        in_specs=[col(cols[0]), col(cols[1]), col(cols[2]),
                  pl.BlockSpec(w.shape, lambda b, t: (0, 0))],
        out_specs=pl.BlockSpec((ts, MIX_W), lambda b, t: (b * nt + t, 0)),
        out_shape=jax.ShapeDtypeStruct((bsz * seq, MIX_W), BF16),
        scratch_shapes=[pltpu.VMEM((SUBLANES, MIX_W), F32)],
        compiler_params=_cparams(("parallel", "arbitrary"), 32),
        name="conv_mixer",
    )(proj, proj, proj, w)


def _lru_kernel(x_ref, gate_ref, cw_ref, cb_ref, wa_ref, ba_ref, wx_ref, bx_ref, lam_ref, o_ref,
                xprev_ref, hprev_ref, a_s, b_s):
    ts = x_ref.shape[0]

    @pl.when(pl.program_id(1) == 0)
    def _():
        xprev_ref[...] = jnp.zeros_like(xprev_ref)
        hprev_ref[...] = jnp.zeros_like(hprev_ref)

    xb = x_ref[...]
    prev = xprev_ref[...]
    cw = cw_ref[...]
    xc = (cw[0:1] * _shift_rows(xb, prev, 3) + cw[1:2] * _shift_rows(xb, prev, 2)
          + cw[2:3] * _shift_rows(xb, prev, 1) + cw[3:4] * xb) + cb_ref[...]
    xprev_ref[...] = xb[ts - SUBLANES:]

    xcb = xc.astype(BF16)
    ra, rx = [], []
    for c in range(MIX_W // LRU_COLS):
        blk = xcb[:, c * LRU_COLS:(c + 1) * LRU_COLS]
        ra.append(_dot(blk, wa_ref[c]))
        rx.append(_dot(blk, wx_ref[c]))
    r_t = jax.nn.sigmoid(jnp.concatenate(ra, axis=1) + ba_ref[...])
    i_t = jax.nn.sigmoid(jnp.concatenate(rx, axis=1) + bx_ref[...])
    log_a = (-LRU_C) * r_t * _softplus(-lam_ref[...])
    th = jnp.tanh(log_a)
    a_s[...] = jnp.exp(log_a)
    b_s[...] = jnp.sqrt(-2.0 * th / (1.0 - th)) * (i_t * xc)

    row = lax.broadcasted_iota(jnp.int32, (SUBLANES, MIX_W), 0)

    def body(g, hp):
        r0 = pl.multiple_of(g * SUBLANES, SUBLANES)
        a = a_s[pl.ds(r0, SUBLANES), :]
        b = b_s[pl.ds(r0, SUBLANES), :]
        for k in (1, 2, 4):
            a_sh = jnp.where(row >= k, pltpu.roll(a, k, 0), 1.0)
            b_sh = jnp.where(row >= k, pltpu.roll(b, k, 0), 0.0)
            b = a * b_sh + b
            a = a * a_sh
        h = a * hp + b
        b_s[pl.ds(r0, SUBLANES), :] = h
        return jnp.broadcast_to(h[SUBLANES - 1:SUBLANES], (SUBLANES, MIX_W))

    hprev_ref[...] = lax.fori_loop(0, ts // SUBLANES, body, hprev_ref[...])
    o_ref[...] = (b_s[...] * _gelu_tanh(gate_ref[...])).astype(o_ref.dtype)


def _lru_mixer(proj, p, bsz, seq, cols, ts=256):
    nt = seq // ts
    col = lambda c: pl.BlockSpec((ts, MIX_W), lambda b, t, c=c: (b * nt + t, c))
    full = lambda a: pl.BlockSpec(a.shape, lambda b, t, n=a.ndim: (0,) * n)
    params = (p["lru_cw"], p["lru_cb"], p["lru_wa"], p["lru_ba"], p["lru_wx"], p["lru_bx"], p["lru_lam"])
    return pl.pallas_call(
        _lru_kernel,
        grid=(bsz, nt),
        in_specs=[col(cols[0]), col(cols[1])] + [full(a) for a in params],
        out_specs=pl.BlockSpec((ts, MIX_W), lambda b, t: (b * nt + t, 0)),
        out_shape=jax.ShapeDtypeStruct((bsz * seq, MIX_W), BF16),
        scratch_shapes=[pltpu.VMEM((SUBLANES, MIX_W), F32), pltpu.VMEM((SUBLANES, MIX_W), F32),
                        pltpu.VMEM((ts, MIX_W), F32), pltpu.VMEM((ts, MIX_W), F32)],
        compiler_params=_cparams(("parallel", "arbitrary"), 32),
        name="lru_mixer",
    )(proj, proj, *params)


def _s5_kernel(u_ref, bw_ref, cw_ref, pw_ref, d_ref, wglu_ref, bglu_ref, o_ref, s_re, s_im, carry_ref):
    ts = u_ref.shape[0]
    nblk = MIX_W // S5_COLS

    @pl.when(pl.program_id(1) == 0)
    def _():
        carry_ref[...] = jnp.zeros_like(carry_ref)

    u = u_ref[...]
    ub = u.astype(BF16)
    ys = []
    for c in range(nblk):
        bu = _dot(ub[:, c * S5_COLS:(c + 1) * S5_COLS], bw_ref[c])
        s_re[...] = bu[:, :S5_LANES]
        s_im[...] = bu[:, S5_LANES:]

        def body(g, hp, c=c):
            hpr, hpi = hp
            r0 = pl.multiple_of(g * SUBLANES, SUBLANES)
            xr = s_re[pl.ds(r0, SUBLANES), :]
            xi = s_im[pl.ds(r0, SUBLANES), :]
            for lvl, k in enumerate((1, 2, 4)):
                mr = pw_ref[c, 2 * lvl]
                mi = pw_ref[c, 2 * lvl + 1]
                sr = pltpu.roll(xr, k, 0)
                si = pltpu.roll(xi, k, 0)
                xr, xi = xr + (mr * sr - mi * si), xi + (mr * si + mi * sr)
            pr = pw_ref[c, 6]
            pi = pw_ref[c, 7]
            hr = xr + (pr * hpr - pi * hpi)
            hi = xi + (pr * hpi + pi * hpr)
            s_re[pl.ds(r0, SUBLANES), :] = hr
            s_im[pl.ds(r0, SUBLANES), :] = hi
            return (jnp.broadcast_to(hr[SUBLANES - 1:SUBLANES], hr.shape),
                    jnp.broadcast_to(hi[SUBLANES - 1:SUBLANES], hi.shape))

        hpr, hpi = lax.fori_loop(0, ts // SUBLANES, body, (carry_ref[c, 0], carry_ref[c, 1]))
        carry_ref[c, 0] = hpr
        carry_ref[c, 1] = hpi
        hb = jnp.concatenate([s_re[...], s_im[...]], axis=1).astype(BF16)
        ys.append(_dot(hb, cw_ref[c]))
    y = jnp.concatenate(ys, axis=1) + d_ref[...] * u
    y = _gelu_tanh(y)
    z = _dot(y.astype(BF16), wglu_ref[...]) + bglu_ref[...]
    o_ref[...] = (y * jax.nn.sigmoid(z)).astype(o_ref.dtype)


def _s5_mixer(proj, p, bsz, seq, col_idx, ts=256):
    nt = seq // ts
    nblk = MIX_W // S5_COLS
    full = lambda a: pl.BlockSpec(a.shape, lambda b, t, n=a.ndim: (0,) * n)
    params = (p["s5_bw"], p["s5_cw"], p["s5_pw"], p["s5_d"], p["s5_wglu"], p["s5_bglu"])
    return pl.pallas_call(
        _s5_kernel,
        grid=(bsz, nt),
        in_specs=[pl.BlockSpec((ts, MIX_W), lambda b, t: (b * nt + t, col_idx))] + [full(a) for a in params],
        out_specs=pl.BlockSpec((ts, MIX_W), lambda b, t: (b * nt + t, 0)),
        out_shape=jax.ShapeDtypeStruct((bsz * seq, MIX_W), BF16),
        scratch_shapes=[pltpu.VMEM((ts, S5_LANES), F32), pltpu.VMEM((ts, S5_LANES), F32),
                        pltpu.VMEM((nblk, 2, SUBLANES, S5_LANES), F32)],
        compiler_params=_cparams(("parallel", "arbitrary"), 48),
        name="s5_mixer",
    )(proj, *params)


def _gla_kernel(q_ref, k_ref, v_ref, r_ref, glr_ref, wgu_ref, bg_ref, gn_ref, o_ref, state_ref):
    ts = q_ref.shape[0]
    dk = q_ref.shape[1] // GLA_HEADS
    dv = v_ref.shape[1] // GLA_HEADS
    qscale = dk ** -0.5

    @pl.when(pl.program_id(1) == 0)
    def _():
        state_ref[...] = jnp.zeros_like(state_ref)

    ri = lax.broadcasted_iota(jnp.int32, (CHUNK, CHUNK), 0)
    ci = lax.broadcasted_iota(jnp.int32, (CHUNK, CHUNK), 1)
    tri = ri >= ci
    tri_ones = jnp.where(tri, 1.0, 0.0).astype(F32)
    gn = gn_ref[...]

    for n in range(ts // CHUNK):
        rows = slice(n * CHUNK, (n + 1) * CHUNK)
        x = _dot(glr_ref[rows, :].astype(BF16), wgu_ref[...]) + bg_ref[...]
        log_a = -_softplus(-x) / GLA_TAU
        bcum = jnp.dot(tri_ones, log_a, preferred_element_type=F32, precision=lax.Precision.HIGHEST)
        b_last = bcum[CHUNK - 1:CHUNK]
        rel = bcum - bcum[CHUNK // 2 - 1:CHUNK // 2]
        e_pos = jnp.exp(rel)
        e_neg = jnp.exp(-rel)
        e_cum = jnp.exp(bcum)
        e_last = jnp.exp(b_last - bcum)
        decay = jnp.exp(b_last)
        for h in range(GLA_HEADS):
            ks = slice(h * dk, (h + 1) * dk)
            vs = slice(h * dv, (h + 1) * dv)
            qh = q_ref[rows, ks] * qscale
            kh = k_ref[rows, ks]
            vh = v_ref[rows, vs].astype(BF16)
            a_low = _dot_nt((qh * e_pos[:, ks]).astype(BF16), (kh * e_neg[:, ks]).astype(BF16))
            a_up = _dot_nt((qh * e_neg[:, ks]).astype(BF16), (kh * e_pos[:, ks]).astype(BF16))
            attn = jnp.where(tri, a_low, a_up)
            o = _dot(attn.astype(BF16), vh)
            st = state_ref[h]
            o = o + _dot_nt((qh * e_cum[:, ks]).astype(BF16), st.astype(BF16))
            kv_t = _dot_tn(vh, (kh * e_last[:, ks]).astype(BF16))
            state_ref[h] = decay[:, ks] * st + kv_t
            o = o * lax.rsqrt(jnp.mean(o * o, axis=-1, keepdims=True) + EPS) * gn
            rg = r_ref[rows, vs]
            o_ref[rows, vs] = (o * (rg * jax.nn.sigmoid(rg))).astype(o_ref.dtype)


def _gla_mixer(proj, glr, p, bsz, seq, cols, ts=256):
    nt = seq // ts
    kw = MIX_W // 2
    full = lambda a: pl.BlockSpec(a.shape, lambda b, t, n=a.ndim: (0,) * n)
    params = (p["gla_wgu"], p["gla_bg"], p["gla_norm"])
    dv = MIX_W // GLA_HEADS
    dk = kw // GLA_HEADS
    return pl.pallas_call(
        _gla_kernel,
        grid=(bsz, nt),
        in_specs=[pl.BlockSpec((ts, kw), lambda b, t: (b * nt + t, cols[0])),
                  pl.BlockSpec((ts, kw), lambda b, t: (b * nt + t, cols[1])),
                  pl.BlockSpec((ts, MIX_W), lambda b, t: (b * nt + t, cols[2])),
                  pl.BlockSpec((ts, MIX_W), lambda b, t: (b * nt + t, cols[3])),
                  pl.BlockSpec((ts, GLA_RANK_PAD), lambda b, t: (b * nt + t, 0))]
                 + [full(a) for a in params],
        out_specs=pl.BlockSpec((ts, MIX_W), lambda b, t: (b * nt + t, 0)),
        out_shape=jax.ShapeDtypeStruct((bsz * seq, MIX_W), BF16),
        scratch_shapes=[pltpu.VMEM((GLA_HEADS, dv, dk), F32)],
        compiler_params=_cparams(("parallel", "arbitrary"), 32),
        name="gla_mixer",
    )(proj, proj, proj, proj, glr, *params)


def _block_diag(blocks):
    n, g, r, c = blocks.shape
    eye = jnp.eye(g, dtype=blocks.dtype)
    return jnp.einsum('ngrc,gk->ngrkc', blocks, eye).reshape(n, g * r, g * c)


def _s5_params(lam_re, lam_im, log_dt, b_re, b_im, c_re, c_im):
    groups = lam_re.shape[0]
    nblk = groups // S5_GPC
    dt = jnp.exp(log_dt)[:, None]
    mag = jnp.exp(lam_re * dt)
    ab_re, ab_im = mag * jnp.cos(lam_im * dt), mag * jnp.sin(lam_im * dt)
    den = lam_re * lam_re + lam_im * lam_im
    nr, ni = ab_re - 1.0, ab_im
    z_re = (nr * lam_re + ni * lam_im) / den
    z_im = (ni * lam_re - nr * lam_im) / den
    bb_re = z_re[..., None] * b_re - z_im[..., None] * b_im
    bb_im = z_re[..., None] * b_im + z_im[..., None] * b_re
    to_blk = lambda a: _block_diag(jnp.swapaxes(a, 1, 2).reshape(nblk, S5_GPC, S5_GROUP, S5_STATE))
    bw = jnp.concatenate([to_blk(bb_re), to_blk(bb_im)], axis=2).astype(BF16)
    from_blk = lambda a: _block_diag(jnp.swapaxes(a, 1, 2).reshape(nblk, S5_GPC, S5_STATE, S5_GROUP))
    cw = jnp.concatenate([from_blk(c_re), from_blk(-c_im)], axis=1).astype(BF16)
    ar = ab_re.reshape(nblk, S5_LANES)
    ai = ab_im.reshape(nblk, S5_LANES)
    pows = [(ar, ai)]
    for _ in range(SUBLANES - 1):
        pr, pi = pows[-1]
        pows.append((pr * ar - pi * ai, pr * ai + pi * ar))
    row = jnp.arange(SUBLANES)[None, :, None]
    planes = []
    for k in (1, 2, 4):
        for comp in (0, 1):
            planes.append(jnp.where(row >= k, pows[k - 1][comp][:, None, :], 0.0))
    planes.append(jnp.stack([pw[0] for pw in pows], axis=1))
    planes.append(jnp.stack([pw[1] for pw in pows], axis=1))
    pw = jnp.stack(planes, axis=1).astype(F32)
    return bw, cw, pw


def _mixer_params(l, a):
    row = lambda v: v.reshape(1, -1).astype(F32)
    p = {}
    p["s5_bw"], p["s5_cw"], p["s5_pw"] = _s5_params(
        a["s5_lam_re"][l], a["s5_lam_im"][l], a["s5_log_dt"][l], a["s5_b_re"][l], a["s5_b_im"][l],
        a["s5_c_re"][l], a["s5_c_im"][l])
    p["s5_d"] = row(a["s5_d"][l])
    p["s5_wglu"] = a["s5_w_glu"][l].astype(BF16)
    p["s5_bglu"] = row(a["s5_b_glu"][l])
    p["gla_wgu"] = jnp.pad(a["gla_w_gate_up"][l].astype(BF16), ((0, GLA_RANK_PAD - GLA_RANK), (0, 0)))
    p["gla_bg"] = row(a["gla_b_gate"][l])
    p["gla_norm"] = row(a["gla_norm"][l])
    p["conv_w"] = a["conv_w"][l].astype(F32)
    p["lru_cw"] = a["lru_conv_w"][l].astype(F32)
    p["lru_cb"] = row(a["lru_conv_b"][l])
    per_blk = LRU_COLS // LRU_BLOCK
    bd = lambda w: _block_diag(w.reshape(-1, per_blk, LRU_BLOCK, LRU_BLOCK)).astype(BF16)
    p["lru_wa"] = bd(a["lru_w_a"][l])
    p["lru_wx"] = bd(a["lru_w_x"][l])
    p["lru_ba"] = row(a["lru_b_a"][l])
    p["lru_bx"] = row(a["lru_b_x"][l])
    p["lru_lam"] = row(a["lru_lam"][l])
    return p


_COL_S5 = 0
_COLS_GLA = (2, 3, 2, 3)
_COLS_CONV = (4, 5, 6)
_COLS_LRU = (7, 8)


def kernel(x, norm_ffn1, ffn1_w_gate, ffn1_w_up, ffn1_w_down, norm_mix, w_in, s5_lam_re, s5_lam_im, s5_log_dt, s5_b_re, s5_b_im, s5_c_re, s5_c_im, s5_d, s5_w_glu, s5_b_glu, gla_w_gate_up, gla_b_gate, gla_norm, conv_w, lru_conv_w, lru_conv_b, lru_w_a, lru_b_a, lru_w_x, lru_b_x, lru_lam, w_br_s5, w_br_gla, w_br_conv, w_br_lru, w_merge, b_merge, w_out, norm_ffn2, ffn2_w_gate, ffn2_w_up, ffn2_w_down, norm_final):
    a = dict(locals())
    bsz, seq, d = x.shape
    depth = norm_ffn1.shape[0]
    glr0 = MIX_W + MIX_W // 2 + MIX_W // 2 + MIX_W
    w_main = jnp.concatenate([w_in[:, :, :glr0], w_in[:, :, glr0 + GLA_RANK:]], axis=2).astype(BF16)
    w_glr = jnp.pad(w_in[:, :, glr0:glr0 + GLA_RANK].astype(BF16), ((0, 0), (0, 0), (0, GLA_RANK_PAD - GLA_RANK)))
    w_merge_bf = w_merge.astype(BF16)
    b_merge_r = b_merge.reshape(depth, -1, 1, d)
    w_brs = [w.astype(BF16) for w in (w_br_s5, w_br_gla, w_br_conv, w_br_lru)]
    w_out_bf = w_out.astype(BF16)

    xf = x.reshape(bsz * seq, d)
    for l in range(depth):
        p = _mixer_params(l, a)
        h = _rmsnorm(xf, norm_ffn1[l], BF16)
        act, wd = _ffn_up(h, ffn1_w_gate, ffn1_w_up, ffn1_w_down, l)
        xf = _mm_res(act, wd, xf, FFN_RES)

        xn = _rmsnorm(xf, norm_mix[l], BF16)
        proj = _matmul(xn, w_main, l, F32, tm=1024, tn=512)
        glr = _matmul(xn, w_glr, l, F32, tm=1024, tn=GLA_RANK_PAD)
        y_s5 = _s5_mixer(proj, p, bsz, seq, _COL_S5)
        y_gla = _gla_mixer(proj, glr, p, bsz, seq, _COLS_GLA)
        y_conv = _conv_mixer(proj, p["conv_w"], bsz, seq, _COLS_CONV)
        y_lru = _lru_mixer(proj, p, bsz, seq, _COLS_LRU)
        merged = _merge(xn, (y_s5, y_gla, y_conv, y_lru), w_merge_bf, b_merge_r, w_brs, l)
        xf = _mm_res(merged, w_out_bf, xf, 1.0, l=l)

        h = _rmsnorm(xf, norm_ffn2[l], BF16)
        act, wd = _ffn_up(h, ffn2_w_gate, ffn2_w_up, ffn2_w_down, l)
        xf = _mm_res(act, wd, xf, FFN_RES)
    return _rmsnorm(xf, norm_final, F32).reshape(bsz, seq, d)
```

```python
import functools
import math

import jax
import jax.numpy as jnp
from jax import lax
from jax.experimental import pallas as pl
from jax.experimental.pallas import tpu as pltpu

F32 = jnp.float32
BF16 = jnp.bfloat16

EPS = 1e-6
CHUNK = 64
FFN_RES = 0.5
SUBLANES = 8
MIB = 2 ** 20
EPI_ROWS = 16
NORM_UNROLL = 4
MM_RES_CHUNKS = 8
WD_SLAB = 16

S5_GROUP = 16
S5_STATE = 64
S5_COLS = 256
S5_GPC = S5_COLS // S5_GROUP
S5_LANES = S5_GPC * S5_STATE
GLA_HEADS = 4
GLA_RANK = 16
GLA_RANK_PAD = 128
GLA_TAU = 16.0
LRU_BLOCK = 64
LRU_C = 8.0
LRU_COLS = 256
MIX_W = 1024


def _cparams(sem, vmem_mib):
    return pltpu.CompilerParams(dimension_semantics=sem, vmem_limit_bytes=vmem_mib * MIB)


def _dot(a, b):
    return jnp.dot(a, b, preferred_element_type=F32)


def _dot_nt(a, b):
    return lax.dot_general(a, b, (((1,), (1,)), ((), ())), preferred_element_type=F32)


def _dot_tn(a, b):
    return lax.dot_general(a, b, (((0,), (0,)), ((), ())), preferred_element_type=F32)


def _softplus(x):
    return jnp.maximum(x, 0.0) + jnp.log1p(jnp.exp(-jnp.abs(x)))


def _gelu_tanh(x):
    return 0.5 * x * (1.0 + jnp.tanh(math.sqrt(2.0 / math.pi) * (x + 0.044715 * (x * x * x))))


def _rms_scale(x, g):
    ms = jnp.mean(x * x, axis=-1, keepdims=True)
    return x * lax.rsqrt(ms + EPS) * g


def _rmsnorm_kernel(x_ref, g_ref, o_ref):
    g = g_ref[...]

    def rows_body(r, carry):
        rows = pl.ds(pl.multiple_of(r * EPI_ROWS, EPI_ROWS), EPI_ROWS)
        o_ref[rows, :] = _rms_scale(x_ref[rows, :], g).astype(o_ref.dtype)
        return carry

    lax.fori_loop(0, x_ref.shape[0] // EPI_ROWS, rows_body, 0, unroll=NORM_UNROLL)


def _rmsnorm(x, g, out_dtype, tm=512):
    t, d = x.shape
    return pl.pallas_call(
        _rmsnorm_kernel,
        grid=(t // tm,),
        in_specs=[pl.BlockSpec((tm, d), lambda i: (i, 0)),
                  pl.BlockSpec((1, d), lambda i: (0, 0))],
        out_specs=pl.BlockSpec((tm, d), lambda i: (i, 0)),
        out_shape=jax.ShapeDtypeStruct((t, d), out_dtype),
        compiler_params=_cparams(("parallel",), 40),
        name="rmsnorm",
    )(x, g.reshape(1, d))


def _matmul_kernel(x_ref, w_ref, o_ref):
    o_ref[...] = _dot(x_ref[...], w_ref[...]).astype(o_ref.dtype)


def _matmul(x, w, l, out_dtype, tm, tn):
    t, k = x.shape
    n = w.shape[2]
    return pl.pallas_call(
        _matmul_kernel,
        grid=(t // tm, n // tn),
        in_specs=[pl.BlockSpec((tm, k), lambda i, j: (i, 0)),
                  pl.BlockSpec((None, k, tn), lambda i, j: (l, 0, j))],
        out_specs=pl.BlockSpec((tm, tn), lambda i, j: (i, j)),
        out_shape=jax.ShapeDtypeStruct((t, n), out_dtype),
        compiler_params=_cparams(("parallel", "arbitrary"), 48),
        name="proj_matmul",
    )(x, w)


def _ffn_up_kernel(h_ref, wg_ref, wu_ref, wd_ref, o_ref, wdb_ref, wg_bf, wu_bf):
    j = pl.program_id(0)
    i = pl.program_id(1)
    sub = wg_ref.shape[0]
    rows = pl.ds(pl.multiple_of(i * sub, sub), sub)
    wg_bf[j % 2, rows, :] = wg_ref[...].astype(BF16)
    wu_bf[j % 2, rows, :] = wu_ref[...].astype(BF16)
    wdb_ref[...] = wd_ref[...].astype(BF16)

    @pl.when(j > 0)
    def _():
        cur = (j + 1) % 2
        h = h_ref[...]
        g = _dot(h, wg_bf[cur])
        u = _dot(h, wu_bf[cur])
        o_ref[...] = (g * jax.nn.sigmoid(g) * u).astype(o_ref.dtype)


def _ffn_up(h, wg, wu, wd, l, tm=1024, tn=512):
    t, d = h.shape
    f = wg.shape[2]
    nj, ni = pl.cdiv(f, tn), t // tm
    sub = d // ni
    assert sub * ni == d and sub % WD_SLAB == 0
    steps = (nj + 1) * ni
    slab_rows = next(r for r in range(WD_SLAB, f + 1, WD_SLAB) if f % r == 0 and f // r <= steps)
    slabs = f // slab_rows
    slab = lambda j, i: jnp.minimum(j * ni + i, slabs - 1)
    row_tile = lambda j, i: jnp.where(j == 0, 0, i)
    return pl.pallas_call(
        _ffn_up_kernel,
        grid=(nj + 1, ni),
        in_specs=[pl.BlockSpec((tm, d), lambda j, i: (row_tile(j, i), 0)),
                  pl.BlockSpec((None, sub, tn), lambda j, i: (l, i, jnp.minimum(j, nj - 1))),
                  pl.BlockSpec((None, sub, tn), lambda j, i: (l, i, jnp.minimum(j, nj - 1))),
                  pl.BlockSpec((None, slab_rows, d), lambda j, i: (l, slab(j, i), 0))],
        out_specs=[pl.BlockSpec((tm, tn), lambda j, i: (row_tile(j, i), jnp.maximum(j - 1, 0))),
                   pl.BlockSpec((slab_rows, d), lambda j, i: (slab(j, i), 0))],
        out_shape=[jax.ShapeDtypeStruct((t, f), BF16),
                   jax.ShapeDtypeStruct((f, d), BF16)],
        scratch_shapes=[pltpu.VMEM((2, d, tn), BF16), pltpu.VMEM((2, d, tn), BF16)],
        compiler_params=_cparams(("arbitrary", "arbitrary"), 56),
        name="ffn_up",
    )(h, wg, wu, wd)


def _mm_res_kernel(a_ref, w_ref, xs_ref, o_ref, acc_ref, xres_ref, res_ref, *, scale, nk, k_last, nchunk, ntiles):
    tile = pl.program_id(0)
    k = pl.program_id(1)
    tk = a_ref.shape[1]
    cr = xs_ref.shape[0]
    kc = jnp.minimum(k, nchunk - 1)
    rows = pl.ds(pl.multiple_of(kc * cr, cr), cr)

    @pl.when(tile > 0)
    def _():
        o_ref[...] = res_ref[rows, :]

    @pl.when(tile < ntiles)
    def _():
        xres_ref[rows, :] = xs_ref[...]

        @pl.when(k == 0)
        def _():
            acc_ref[...] = _dot(a_ref[...], w_ref[...])

        @pl.when((k > 0) & (k < nk - 1))
        def _():
            acc_ref[...] += _dot(a_ref[...], w_ref[...])

        @pl.when(k == nk - 1)
        def _():
            if k_last == tk:
                part = _dot(a_ref[...], w_ref[...])
            else:
                part = _dot(a_ref[:, :k_last], w_ref[:k_last, :])
            res_ref[...] = xres_ref[...] + scale * (acc_ref[...] + part)


def _mm_res(a, w, xres, scale, l=None, tm=1024, tn=2048, tk=1024):
    t, kdim = a.shape
    n = w.shape[-1]
    nk = pl.cdiv(kdim, tk)
    k_last = kdim - (nk - 1) * tk
    ni, nj = t // tm, n // tn
    ntiles = ni * nj
    nchunk = min(nk, MM_RES_CHUNKS)
    cr = tm // nchunk
    assert nk >= 2 and cr * nchunk == tm and cr % SUBLANES == 0
    cur = lambda tile: jnp.minimum(tile, ntiles - 1)
    prev = lambda tile: jnp.maximum(tile - 1, 0)
    k_blk = lambda tile, k: jnp.where(tile == ntiles, nk - 1, k)
    chunk = lambda k: jnp.minimum(k, nchunk - 1)
    out_chunk = lambda tile, k: jnp.where(tile == 0, 0, chunk(k))
    if l is None:
        w_spec = pl.BlockSpec((tk, tn), lambda tile, k: (k_blk(tile, k), cur(tile) % nj))
    else:
        w_spec = pl.BlockSpec((None, tk, tn), lambda tile, k: (l, k_blk(tile, k), cur(tile) % nj))
    return pl.pallas_call(
        functools.partial(_mm_res_kernel, scale=scale, nk=nk, k_last=k_last, nchunk=nchunk, ntiles=ntiles),
        grid=(ntiles + 1, nk),
        in_specs=[pl.BlockSpec((tm, tk), lambda tile, k: (cur(tile) // nj, k_blk(tile, k))),
                  w_spec,
                  pl.BlockSpec((cr, tn), lambda tile, k: ((cur(tile) // nj) * nchunk + chunk(k), cur(tile) % nj))],
        out_specs=pl.BlockSpec((cr, tn), lambda tile, k: ((prev(tile) // nj) * nchunk + out_chunk(tile, k),
                                                          prev(tile) % nj)),
        out_shape=jax.ShapeDtypeStruct((t, n), F32),
        scratch_shapes=[pltpu.VMEM((tm, tn), F32), pltpu.VMEM((tm, tn), F32), pltpu.VMEM((tm, tn), F32)],
        compiler_params=_cparams(("arbitrary", "arbitrary"), 56),
        name="mm_res",
    )(a, w, xres)


def _merge_kernel(xn_ref, y0_ref, y1_ref, y2_ref, y3_ref, wm_ref, bm_ref, w0_ref, w1_ref, w2_ref, w3_ref, o_ref,
                  wm_bf, wb_bf):
    j = pl.program_id(0)
    i = pl.program_id(1)
    sub_m, sub_b = wm_ref.shape[1], w0_ref.shape[0]
    rows_m = pl.ds(pl.multiple_of(i * sub_m, sub_m), sub_m)
    rows_b = pl.ds(pl.multiple_of(i * sub_b, sub_b), sub_b)
    for n, wb_ref in enumerate((w0_ref, w1_ref, w2_ref, w3_ref)):
        wm_bf[j % 2, n, rows_m, :] = wm_ref[n].astype(BF16)
        wb_bf[j % 2, n, rows_b, :] = wb_ref[...].astype(BF16)

    @pl.when(j > 0)
    def _():
        cur = (j + 1) % 2
        xn = xn_ref[...]
        acc = None
        for n, y_ref in enumerate((y0_ref, y1_ref, y2_ref, y3_ref)):
            gate = jax.nn.sigmoid(_dot(xn, wm_bf[cur, n]) + bm_ref[n])
            term = gate * _dot(y_ref[...], wb_bf[cur, n])
            acc = term if acc is None else acc + term
        o_ref[...] = acc.astype(o_ref.dtype)


def _merge(xn, ys, wm, bm, wbs, l, tm=512, tn=256):
    t, d = xn.shape
    nb = wm.shape[1]
    w = wbs[0].shape[1]
    nj, ni = d // tn, t // tm
    sub_m, sub_b = d // ni, w // ni
    assert sub_m * ni == d and sub_b * ni == w and sub_b % WD_SLAB == 0
    row_tile = lambda j, i: jnp.where(j == 0, 0, i)
    stage_col = lambda j: jnp.minimum(j, nj - 1)
    out_col = lambda j: jnp.maximum(j - 1, 0)
    y_spec = pl.BlockSpec((tm, w), lambda j, i: (row_tile(j, i), 0))
    wb_spec = pl.BlockSpec((None, sub_b, tn), lambda j, i: (l, i, stage_col(j)))
    return pl.pallas_call(
        _merge_kernel,
        grid=(nj + 1, ni),
        in_specs=[pl.BlockSpec((tm, d), lambda j, i: (row_tile(j, i), 0)),
                  y_spec, y_spec, y_spec, y_spec,
                  pl.BlockSpec((None, nb, sub_m, tn), lambda j, i: (l, 0, i, stage_col(j))),
                  pl.BlockSpec((None, nb, 1, tn), lambda j, i: (l, 0, 0, out_col(j))),
                  wb_spec, wb_spec, wb_spec, wb_spec],
        out_specs=pl.BlockSpec((tm, tn), lambda j, i: (row_tile(j, i), out_col(j))),
        out_shape=jax.ShapeDtypeStruct((t, d), BF16),
        scratch_shapes=[pltpu.VMEM((2, nb, d, tn), BF16), pltpu.VMEM((2, nb, w, tn), BF16)],
        compiler_params=_cparams(("arbitrary", "arbitrary"), 56),
        name="merge",
    )(xn, *ys, wm, bm, *wbs)


def _shift_rows(x, prev8, k):
    xs = pltpu.roll(x, k, 0)
    ps = pltpu.roll(prev8, k, 0)
    row = lax.broadcasted_iota(jnp.int32, prev8.shape, 0)
    head = jnp.where(row < k, ps, xs[:SUBLANES])
    return jnp.concatenate([head, xs[SUBLANES:]], axis=0)


def _conv_kernel(h_ref, b_ref, c_ref, w_ref, o_ref, carry_ref):
    @pl.when(pl.program_id(1) == 0)
    def _():
        carry_ref[...] = jnp.zeros_like(carry_ref)

    z = c_ref[...] * h_ref[...]
    prev = carry_ref[...]
    w = w_ref[...]
    y = w[0:1] * _shift_rows(z, prev, 2) + w[1:2] * _shift_rows(z, prev, 1) + w[2:3] * z
    o_ref[...] = (b_ref[...] * y).astype(o_ref.dtype)
    carry_ref[...] = z[z.shape[0] - SUBLANES:]


def _conv_mixer(proj, w, bsz, seq, cols, ts=512):
    nt = seq // ts
    col = lambda c: pl.BlockSpec((ts, MIX_W), lambda b, t, c=c: (b * nt + t, c))
    return pl.pallas_call(
        _conv_kernel,
        grid=(bsz, nt),
        in_specs=[col(cols[0]), col(cols[1]), col(cols[2]),
                  pl.BlockSpec(w.shape, lambda b, t: (0, 0))],
        out_specs=pl.BlockSpec((ts, MIX_W), lambda b, t: (b * nt + t, 0)),
        out_shape=jax.ShapeDtypeStruct((bsz * seq, MIX_W), BF16),
        scratch_shapes=[pltpu.VMEM((SUBLANES, MIX_W), F32)],
        compiler_params=_cparams(("parallel", "arbitrary"), 32),
        name="conv_mixer",
    )(proj, proj, proj, w)


def _lru_kernel(x_ref, gate_ref, cw_ref, cb_ref, wa_ref, ba_ref, wx_ref, bx_ref, lam_ref, o_ref,
                xprev_ref, hprev_ref, a_s, b_s):
    ts = x_ref.shape[0]

    @pl.when(pl.program_id(1) == 0)
    def _():
        xprev_ref[...] = jnp.zeros_like(xprev_ref)
        hprev_ref[...] = jnp.zeros_like(hprev_ref)

    xb = x_ref[...]
    prev = xprev_ref[...]
    cw = cw_ref[...]
    xc = (cw[0:1] * _shift_rows(xb, prev, 3) + cw[1:2] * _shift_rows(xb, prev, 2)
          + cw[2:3] * _shift_rows(xb, prev, 1) + cw[3:4] * xb) + cb_ref[...]
    xprev_ref[...] = xb[ts - SUBLANES:]

    xcb = xc.astype(BF16)
    ra, rx = [], []
    for c in range(MIX_W // LRU_COLS):
        blk = xcb[:, c * LRU_COLS:(c + 1) * LRU_COLS]
        ra.append(_dot(blk, wa_ref[c]))
        rx.append(_dot(blk, wx_ref[c]))
    r_t = jax.nn.sigmoid(jnp.concatenate(ra, axis=1) + ba_ref[...])
    i_t = jax.nn.sigmoid(jnp.concatenate(rx, axis=1) + bx_ref[...])
    log_a = (-LRU_C) * r_t * _softplus(-lam_ref[...])
    th = jnp.tanh(log_a)
    a_s[...] = jnp.exp(log_a)
    b_s[...] = jnp.sqrt(-2.0 * th / (1.0 - th)) * (i_t * xc)

    row = lax.broadcasted_iota(jnp.int32, (SUBLANES, MIX_W), 0)

    def body(g, hp):
        r0 = pl.multiple_of(g * SUBLANES, SUBLANES)
        a = a_s[pl.ds(r0, SUBLANES), :]
        b = b_s[pl.ds(r0, SUBLANES), :]
        for k in (1, 2, 4):
            a_sh = jnp.where(row >= k, pltpu.roll(a, k, 0), 1.0)
            b_sh = jnp.where(row >= k, pltpu.roll(b, k, 0), 0.0)
            b = a * b_sh + b
            a = a * a_sh
        h = a * hp + b
        b_s[pl.ds(r0, SUBLANES), :] = h
        return jnp.broadcast_to(h[SUBLANES - 1:SUBLANES], (SUBLANES, MIX_W))

    hprev_ref[...] = lax.fori_loop(0, ts // SUBLANES, body, hprev_ref[...])
    o_ref[...] = (b_s[...] * _gelu_tanh(gate_ref[...])).astype(o_ref.dtype)


def _lru_mixer(proj, p, bsz, seq, cols, ts=256):
    nt = seq // ts
    col = lambda c: pl.BlockSpec((ts, MIX_W), lambda b, t, c=c: (b * nt + t, c))
    full = lambda a: pl.BlockSpec(a.shape, lambda b, t, n=a.ndim: (0,) * n)
    params = (p["lru_cw"], p["lru_cb"], p["lru_wa"], p["lru_ba"], p["lru_wx"], p["lru_bx"], p["lru_lam"])
    return pl.pallas_call(
        _lru_kernel,
        grid=(bsz, nt),
        in_specs=[col(cols[0]), col(cols[1])] + [full(a) for a in params],
        out_specs=pl.BlockSpec((ts, MIX_W), lambda b, t: (b * nt + t, 0)),
        out_shape=jax.ShapeDtypeStruct((bsz * seq, MIX_W), BF16),
        scratch_shapes=[pltpu.VMEM((SUBLANES, MIX_W), F32), pltpu.VMEM((SUBLANES, MIX_W), F32),
                        pltpu.VMEM((ts, MIX_W), F32), pltpu.VMEM((ts, MIX_W), F32)],
        compiler_params=_cparams(("parallel", "arbitrary"), 32),
        name="lru_mixer",
    )(proj, proj, *params)


def _s5_kernel(u_ref, bw_ref, cw_ref, pw_ref, d_ref, wglu_ref, bglu_ref, o_ref, s_re, s_im, carry_ref):
    ts = u_ref.shape[0]
    nblk = MIX_W // S5_COLS

    @pl.when(pl.program_id(1) == 0)
    def _():
        carry_ref[...] = jnp.zeros_like(carry_ref)

    u = u_ref[...]
    ub = u.astype(BF16)
    ys = []
    for c in range(nblk):
        bu = _dot(ub[:, c * S5_COLS:(c + 1) * S5_COLS], bw_ref[c])
        s_re[...] = bu[:, :S5_LANES]
        s_im[...] = bu[:, S5_LANES:]

        def body(g, hp, c=c):
            hpr, hpi = hp
            r0 = pl.multiple_of(g * SUBLANES, SUBLANES)
            xr = s_re[pl.ds(r0, SUBLANES), :]
            xi = s_im[pl.ds(r0, SUBLANES), :]
            for lvl, k in enumerate((1, 2, 4)):
                mr = pw_ref[c, 2 * lvl]
                mi = pw_ref[c, 2 * lvl + 1]
                sr = pltpu.roll(xr, k, 0)
                si = pltpu.roll(xi, k, 0)
                xr, xi = xr + (mr * sr - mi * si), xi + (mr * si + mi * sr)
            pr = pw_ref[c, 6]
            pi = pw_ref[c, 7]
            hr = xr + (pr * hpr - pi * hpi)
            hi = xi + (pr * hpi + pi * hpr)
            s_re[pl.ds(r0, SUBLANES), :] = hr
            s_im[pl.ds(r0, SUBLANES), :] = hi
            return (jnp.broadcast_to(hr[SUBLANES - 1:SUBLANES], hr.shape),
                    jnp.broadcast_to(hi[SUBLANES - 1:SUBLANES], hi.shape))

        hpr, hpi = lax.fori_loop(0, ts // SUBLANES, body, (carry_ref[c, 0], carry_ref[c, 1]))
        carry_ref[c, 0] = hpr
        carry_ref[c, 1] = hpi
        hb = jnp.concatenate([s_re[...], s_im[...]], axis=1).astype(BF16)
        ys.append(_dot(hb, cw_ref[c]))
    y = jnp.concatenate(ys, axis=1) + d_ref[...] * u
    y = _gelu_tanh(y)
    z = _dot(y.astype(BF16), wglu_ref[...]) + bglu_ref[...]
    o_ref[...] = (y * jax.nn.sigmoid(z)).astype(o_ref.dtype)


def _s5_mixer(proj, p, bsz, seq, col_idx, ts=256):
    nt = seq // ts
    nblk = MIX_W // S5_COLS
    full = lambda a: pl.BlockSpec(a.shape, lambda b, t, n=a.ndim: (0,) * n)
    params = (p["s5_bw"], p["s5_cw"], p["s5_pw"], p["s5_d"], p["s5_wglu"], p["s5_bglu"])
    return pl.pallas_call(
        _s5_kernel,
        grid=(bsz, nt),
        in_specs=[pl.BlockSpec((ts, MIX_W), lambda b, t: (b * nt + t, col_idx))] + [full(a) for a in params],
        out_specs=pl.BlockSpec((ts, MIX_W), lambda b, t: (b * nt + t, 0)),
        out_shape=jax.ShapeDtypeStruct((bsz * seq, MIX_W), BF16),
        scratch_shapes=[pltpu.VMEM((ts, S5_LANES), F32), pltpu.VMEM((ts, S5_LANES), F32),
                        pltpu.VMEM((nblk, 2, SUBLANES, S5_LANES), F32)],
        compiler_params=_cparams(("parallel", "arbitrary"), 48),
        name="s5_mixer",
    )(proj, *params)


def _gla_kernel(q_ref, k_ref, v_ref, r_ref, glr_ref, wgu_ref, bg_ref, gn_ref, o_ref, state_ref):
    ts = q_ref.shape[0]
    dk = q_ref.shape[1] // GLA_HEADS
    dv = v_ref.shape[1] // GLA_HEADS
    qscale = dk ** -0.5

    @pl.when(pl.program_id(1) == 0)
    def _():
        state_ref[...] = jnp.zeros_like(state_ref)

    ri = lax.broadcasted_iota(jnp.int32, (CHUNK, CHUNK), 0)
    ci = lax.broadcasted_iota(jnp.int32, (CHUNK, CHUNK), 1)
    tri = ri >= ci
    tri_ones = jnp.where(tri, 1.0, 0.0).astype(F32)
    gn = gn_ref[...]

    for n in range(ts // CHUNK):
        rows = slice(n * CHUNK, (n + 1) * CHUNK)
        x = _dot(glr_ref[rows, :].astype(BF16), wgu_ref[...]) + bg_ref[...]
        log_a = -_softplus(-x) / GLA_TAU
        bcum = jnp.dot(tri_ones, log_a, preferred_element_type=F32, precision=lax.Precision.HIGHEST)
        b_last = bcum[CHUNK - 1:CHUNK]
        rel = bcum - bcum[CHUNK // 2 - 1:CHUNK // 2]
        e_pos = jnp.exp(rel)
        e_neg = jnp.exp(-rel)
        e_cum = jnp.exp(bcum)
        e_last = jnp.exp(b_last - bcum)
        decay = jnp.exp(b_last)
        for h in range(GLA_HEADS):
            ks = slice(h * dk, (h + 1) * dk)
            vs = slice(h * dv, (h + 1) * dv)
            qh = q_ref[rows, ks] * qscale
            kh = k_ref[rows, ks]
            vh = v_ref[rows, vs].astype(BF16)
            a_low = _dot_nt((qh * e_pos[:, ks]).astype(BF16), (kh * e_neg[:, ks]).astype(BF16))
            a_up = _dot_nt((qh * e_neg[:, ks]).astype(BF16), (kh * e_pos[:, ks]).astype(BF16))
            attn = jnp.where(tri, a_low, a_up)
            o = _dot(attn.astype(BF16), vh)
            st = state_ref[h]
            o = o + _dot_nt((qh * e_cum[:, ks]).astype(BF16), st.astype(BF16))
            kv_t = _dot_tn(vh, (kh * e_last[:, ks]).astype(BF16))
            state_ref[h] = decay[:, ks] * st + kv_t
            o = o * lax.rsqrt(jnp.mean(o * o, axis=-1, keepdims=True) + EPS) * gn
            rg = r_ref[rows, vs]
            o_ref[rows, vs] = (o * (rg * jax.nn.sigmoid(rg))).astype(o_ref.dtype)


def _gla_mixer(proj, glr, p, bsz, seq, cols, ts=256):
    nt = seq // ts
    kw = MIX_W // 2
    full = lambda a: pl.BlockSpec(a.shape, lambda b, t, n=a.ndim: (0,) * n)
    params = (p["gla_wgu"], p["gla_bg"], p["gla_norm"])
    dv = MIX_W // GLA_HEADS
    dk = kw // GLA_HEADS
    return pl.pallas_call(
        _gla_kernel,
        grid=(bsz, nt),
        in_specs=[pl.BlockSpec((ts, kw), lambda b, t: (b * nt + t, cols[0])),
                  pl.BlockSpec((ts, kw), lambda b, t: (b * nt + t, cols[1])),
                  pl.BlockSpec((ts, MIX_W), lambda b, t: (b * nt + t, cols[2])),
                  pl.BlockSpec((ts, MIX_W), lambda b, t: (b * nt + t, cols[3])),
                  pl.BlockSpec((ts, GLA_RANK_PAD), lambda b, t: (b * nt + t, 0))]
                 + [full(a) for a in params],
        out_specs=pl.BlockSpec((ts, MIX_W), lambda b, t: (b * nt + t, 0)),
        out_shape=jax.ShapeDtypeStruct((bsz * seq, MIX_W), BF16),
        scratch_shapes=[pltpu.VMEM((GLA_HEADS, dv, dk), F32)],
        compiler_params=_cparams(("parallel", "arbitrary"), 32),
        name="gla_mixer",
    )(proj, proj, proj, proj, glr, *params)


def _block_diag(blocks):
    n, g, r, c = blocks.shape
    eye = jnp.eye(g, dtype=blocks.dtype)
    return jnp.einsum('ngrc,gk->ngrkc', blocks, eye).reshape(n, g * r, g * c)


def _s5_params(lam_re, lam_im, log_dt, b_re, b_im, c_re, c_im):
    groups = lam_re.shape[0]
    nblk = groups // S5_GPC
    dt = jnp.exp(log_dt)[:, None]
    mag = jnp.exp(lam_re * dt)
    ab_re, ab_im = mag * jnp.cos(lam_im * dt), mag * jnp.sin(lam_im * dt)
    den = lam_re * lam_re + lam_im * lam_im
    nr, ni = ab_re - 1.0, ab_im
    z_re = (nr * lam_re + ni * lam_im) / den
    z_im = (ni * lam_re - nr * lam_im) / den
    bb_re = z_re[..., None] * b_re - z_im[..., None] * b_im
    bb_im = z_re[..., None] * b_im + z_im[..., None] * b_re
    to_blk = lambda a: _block_diag(jnp.swapaxes(a, 1, 2).reshape(nblk, S5_GPC, S5_GROUP, S5_STATE))
    bw = jnp.concatenate([to_blk(bb_re), to_blk(bb_im)], axis=2).astype(BF16)
    from_blk = lambda a: _block_diag(jnp.swapaxes(a, 1, 2).reshape(nblk, S5_GPC, S5_STATE, S5_GROUP))
    cw = jnp.concatenate([from_blk(c_re), from_blk(-c_im)], axis=1).astype(BF16)
    ar = ab_re.reshape(nblk, S5_LANES)
    ai = ab_im.reshape(nblk, S5_LANES)
    pows = [(ar, ai)]
    for _ in range(SUBLANES - 1):
        pr, pi = pows[-1]
        pows.append((pr * ar - pi * ai, pr * ai + pi * ar))
    row = jnp.arange(SUBLANES)[None, :, None]
    planes = []
    for k in (1, 2, 4):
        for comp in (0, 1):
            planes.append(jnp.where(row >= k, pows[k - 1][comp][:, None, :], 0.0))
    planes.append(jnp.stack([pw[0] for pw in pows], axis=1))
    planes.append(jnp.stack([pw[1] for pw in pows], axis=1))
    pw = jnp.stack(planes, axis=1).astype(F32)
    return bw, cw, pw


def _mixer_params(l, a):
    row = lambda v: v.reshape(1, -1).astype(F32)
    p = {}
    p["s5_bw"], p["s5_cw"], p["s5_pw"] = _s5_params(
        a["s5_lam_re"][l], a["s5_lam_im"][l], a["s5_log_dt"][l], a["s5_b_re"][l], a["s5_b_im"][l],
        a["s5_c_re"][l], a["s5_c_im"][l])
    p["s5_d"] = row(a["s5_d"][l])
    p["s5_wglu"] = a["s5_w_glu"][l].astype(BF16)
    p["s5_bglu"] = row(a["s5_b_glu"][l])
    p["gla_wgu"] = jnp.pad(a["gla_w_gate_up"][l].astype(BF16), ((0, GLA_RANK_PAD - GLA_RANK), (0, 0)))
    p["gla_bg"] = row(a["gla_b_gate"][l])
    p["gla_norm"] = row(a["gla_norm"][l])
    p["conv_w"] = a["conv_w"][l].astype(F32)
    p["lru_cw"] = a["lru_conv_w"][l].astype(F32)
    p["lru_cb"] = row(a["lru_conv_b"][l])
    per_blk = LRU_COLS // LRU_BLOCK
    bd = lambda w: _block_diag(w.reshape(-1, per_blk, LRU_BLOCK, LRU_BLOCK)).astype(BF16)
    p["lru_wa"] = bd(a["lru_w_a"][l])
    p["lru_wx"] = bd(a["lru_w_x"][l])
    p["lru_ba"] = row(a["lru_b_a"][l])
    p["lru_bx"] = row(a["lru_b_x"][l])
    p["lru_lam"] = row(a["lru_lam"][l])
    return p


_COL_S5 = 0
_COLS_GLA = (2, 3, 2, 3)
_COLS_CONV = (4, 5, 6)
_COLS_LRU = (7, 8)


def kernel(x, norm_ffn1, ffn1_w_gate, ffn1_w_up, ffn1_w_down, norm_mix, w_in, s5_lam_re, s5_lam_im, s5_log_dt, s5_b_re, s5_b_im, s5_c_re, s5_c_im, s5_d, s5_w_glu, s5_b_glu, gla_w_gate_up, gla_b_gate, gla_norm, conv_w, lru_conv_w, lru_conv_b, lru_w_a, lru_b_a, lru_w_x, lru_b_x, lru_lam, w_br_s5, w_br_gla, w_br_conv, w_br_lru, w_merge, b_merge, w_out, norm_ffn2, ffn2_w_gate, ffn2_w_up, ffn2_w_down, norm_final):
    a = dict(locals())
    bsz, seq, d = x.shape
    depth = norm_ffn1.shape[0]
    glr0 = MIX_W + MIX_W // 2 + MIX_W // 2 + MIX_W
    w_main = jnp.concatenate([w_in[:, :, :glr0], w_in[:, :, glr0 + GLA_RANK:]], axis=2).astype(BF16)
    w_glr = jnp.pad(w_in[:, :, glr0:glr0 + GLA_RANK].astype(BF16), ((0, 0), (0, 0), (0, GLA_RANK_PAD - GLA_RANK)))
    b_merge_r = b_merge.reshape(depth, -1, 1, d)
    w_brs = (w_br_s5, w_br_gla, w_br_conv, w_br_lru)
    w_out_bf = w_out.astype(BF16)

    xf = x.reshape(bsz * seq, d)
    for l in range(depth):
        p = _mixer_params(l, a)
        h = _rmsnorm(xf, norm_ffn1[l], BF16)
        act, wd = _ffn_up(h, ffn1_w_gate, ffn1_w_up, ffn1_w_down, l)
        xf = _mm_res(act, wd, xf, FFN_RES)

        xn = _rmsnorm(xf, norm_mix[l], BF16)
        proj = _matmul(xn, w_main, l, F32, tm=1024, tn=1024)
        glr = _matmul(xn, w_glr, l, F32, tm=1024, tn=GLA_RANK_PAD)
        y_s5 = _s5_mixer(proj, p, bsz, seq, _COL_S5)
        y_gla = _gla_mixer(proj, glr, p, bsz, seq, _COLS_GLA)
        y_conv = _conv_mixer(proj, p["conv_w"], bsz, seq, _COLS_CONV)
        y_lru = _lru_mixer(proj, p, bsz, seq, _COLS_LRU)
        merged = _merge(xn, (y_s5, y_gla, y_conv, y_lru), w_merge, b_merge_r, w_brs, l)
        xf = _mm_res(merged, w_out_bf, xf, 1.0, l=l)

        h = _rmsnorm(xf, norm_ffn2[l], BF16)
        act, wd = _ffn_up(h, ffn2_w_gate, ffn2_w_up, ffn2_w_down, l)
        xf = _mm_res(act, wd, xf, FFN_RES)
    return _rmsnorm(xf, norm_final, F32).reshape(bsz, seq, d)
```

```python
import functools
import math

import jax
import jax.numpy as jnp
from jax import lax
from jax.experimental import pallas as pl
from jax.experimental.pallas import tpu as pltpu

F32 = jnp.float32
BF16 = jnp.bfloat16

EPS = 1e-6
CHUNK = 64
FFN_RES = 0.5
SUBLANES = 8
MIB = 2 ** 20
EPI_ROWS = 16
NORM_UNROLL = 4
MM_RES_CHUNKS = 8
WD_SLAB = 16

S5_GROUP = 16
S5_STATE = 64
S5_COLS = 256
S5_GPC = S5_COLS // S5_GROUP
S5_LANES = S5_GPC * S5_STATE
GLA_HEADS = 4
GLA_RANK = 16
GLA_RANK_PAD = 128
GLA_TAU = 16.0
LRU_BLOCK = 64
LRU_C = 8.0
LRU_COLS = 256
MIX_W = 1024


def _cparams(sem, vmem_mib):
    return pltpu.CompilerParams(dimension_semantics=sem, vmem_limit_bytes=vmem_mib * MIB)


def _dot(a, b):
    return jnp.dot(a, b, preferred_element_type=F32)


def _dot_nt(a, b):
    return lax.dot_general(a, b, (((1,), (1,)), ((), ())), preferred_element_type=F32)


def _dot_tn(a, b):
    return lax.dot_general(a, b, (((0,), (0,)), ((), ())), preferred_element_type=F32)


def _softplus(x):
    return jnp.maximum(x, 0.0) + jnp.log1p(jnp.exp(-jnp.abs(x)))


def _gelu_tanh(x):
    return 0.5 * x * (1.0 + jnp.tanh(math.sqrt(2.0 / math.pi) * (x + 0.044715 * (x * x * x))))


def _rms_scale(x, g):
    ms = jnp.mean(x * x, axis=-1, keepdims=True)
    return x * lax.rsqrt(ms + EPS) * g


def _rmsnorm_kernel(x_ref, g_ref, o_ref):
    g = g_ref[...]

    def rows_body(r, carry):
        rows = pl.ds(pl.multiple_of(r * EPI_ROWS, EPI_ROWS), EPI_ROWS)
        o_ref[rows, :] = _rms_scale(x_ref[rows, :], g).astype(o_ref.dtype)
        return carry

    lax.fori_loop(0, x_ref.shape[0] // EPI_ROWS, rows_body, 0, unroll=NORM_UNROLL)


def _rmsnorm(x, g, out_dtype, tm=512):
    t, d = x.shape
    return pl.pallas_call(
        _rmsnorm_kernel,
        grid=(t // tm,),
        in_specs=[pl.BlockSpec((tm, d), lambda i: (i, 0)),
                  pl.BlockSpec((1, d), lambda i: (0, 0))],
        out_specs=pl.BlockSpec((tm, d), lambda i: (i, 0)),
        out_shape=jax.ShapeDtypeStruct((t, d), out_dtype),
        compiler_params=_cparams(("parallel",), 40),
        name="rmsnorm",
    )(x, g.reshape(1, d))


def _matmul_kernel(x_ref, w_ref, o_ref):
    o_ref[...] = _dot(x_ref[...], w_ref[...]).astype(o_ref.dtype)


def _matmul(x, w, l, out_dtype, tm, tn):
    t, k = x.shape
    n = w.shape[2]
    return pl.pallas_call(
        _matmul_kernel,
        grid=(t // tm, n // tn),
        in_specs=[pl.BlockSpec((tm, k), lambda i, j: (i, 0)),
                  pl.BlockSpec((None, k, tn), lambda i, j: (l, 0, j))],
        out_specs=pl.BlockSpec((tm, tn), lambda i, j: (i, j)),
        out_shape=jax.ShapeDtypeStruct((t, n), out_dtype),
        compiler_params=_cparams(("parallel", "arbitrary"), 48),
        name="proj_matmul",
    )(x, w)


def _matmul_res_kernel(x_ref, w_ref, r_ref, o_ref):
    o_ref[...] = r_ref[...] + _dot(x_ref[...], w_ref[...])


def _matmul_res(x, w, l, res, tm=1024, tn=1024):
    t, k = x.shape
    n = w.shape[2]
    return pl.pallas_call(
        _matmul_res_kernel,
        grid=(t // tm, n // tn),
        in_specs=[pl.BlockSpec((tm, k), lambda i, j: (i, 0)),
                  pl.BlockSpec((None, k, tn), lambda i, j: (l, 0, j)),
                  pl.BlockSpec((tm, tn), lambda i, j: (i, j))],
        out_specs=pl.BlockSpec((tm, tn), lambda i, j: (i, j)),
        out_shape=jax.ShapeDtypeStruct((t, n), F32),
        compiler_params=_cparams(("parallel", "arbitrary"), 56),
        name="matmul_res",
    )(x, w, res)


def _ffn_up_kernel(h_ref, wg_ref, wu_ref, wd_ref, o_ref, wdb_ref, wg_bf, wu_bf, *, nj, n_last):
    j = pl.program_id(0)
    i = pl.program_id(1)
    sub = wg_ref.shape[0]
    rows = pl.ds(pl.multiple_of(i * sub, sub), sub)
    wg_bf[j % 2, rows, :] = wg_ref[...].astype(BF16)
    wu_bf[j % 2, rows, :] = wu_ref[...].astype(BF16)
    wdb_ref[...] = wd_ref[...].astype(BF16)
    cur = (j + 1) % 2

    def tile(ncols):
        h = h_ref[...]
        g = _dot(h, wg_bf[cur, :, :ncols])
        u = _dot(h, wu_bf[cur, :, :ncols])
        o_ref[:, :ncols] = (g * jax.nn.sigmoid(g) * u).astype(o_ref.dtype)

    @pl.when((j > 0) & (j < nj))
    def _():
        tile(o_ref.shape[1])

    @pl.when(j == nj)
    def _():
        tile(n_last)


def _ffn_up(h, wg, wu, wd, l, tm=1024, tn=512):
    t, d = h.shape
    f = wg.shape[2]
    nj, ni = pl.cdiv(f, tn), t // tm
    sub = d // ni
    assert sub * ni == d and sub % WD_SLAB == 0
    steps = (nj + 1) * ni
    slab_rows = next(r for r in range(WD_SLAB, f + 1, WD_SLAB) if f % r == 0 and f // r <= steps)
    slabs = f // slab_rows
    slab = lambda j, i: jnp.minimum(j * ni + i, slabs - 1)
    row_tile = lambda j, i: jnp.where(j == 0, 0, i)
    return pl.pallas_call(
        functools.partial(_ffn_up_kernel, nj=nj, n_last=f - (nj - 1) * tn),
        grid=(nj + 1, ni),
        in_specs=[pl.BlockSpec((tm, d), lambda j, i: (row_tile(j, i), 0)),
                  pl.BlockSpec((None, sub, tn), lambda j, i: (l, i, jnp.minimum(j, nj - 1))),
                  pl.BlockSpec((None, sub, tn), lambda j, i: (l, i, jnp.minimum(j, nj - 1))),
                  pl.BlockSpec((None, slab_rows, d), lambda j, i: (l, slab(j, i), 0))],
        out_specs=[pl.BlockSpec((tm, tn), lambda j, i: (row_tile(j, i), jnp.maximum(j - 1, 0))),
                   pl.BlockSpec((slab_rows, d), lambda j, i: (slab(j, i), 0))],
        out_shape=[jax.ShapeDtypeStruct((t, f), BF16),
                   jax.ShapeDtypeStruct((f, d), BF16)],
        scratch_shapes=[pltpu.VMEM((2, d, tn), BF16), pltpu.VMEM((2, d, tn), BF16)],
        compiler_params=_cparams(("arbitrary", "arbitrary"), 56),
        name="ffn_up",
    )(h, wg, wu, wd)


def _mm_res_kernel(a_ref, w_ref, xs_ref, o_ref, acc_ref, xres_ref, res_ref, *, scale, nk, k_last, nchunk, ntiles):
    tile = pl.program_id(0)
    k = pl.program_id(1)
    tk = a_ref.shape[1]
    cr = xs_ref.shape[0]
    kc = jnp.minimum(k, nchunk - 1)
    rows = pl.ds(pl.multiple_of(kc * cr, cr), cr)

    @pl.when(tile > 0)
    def _():
        o_ref[...] = res_ref[rows, :]

    @pl.when(tile < ntiles)
    def _():
        xres_ref[rows, :] = xs_ref[...]

        @pl.when(k == 0)
        def _():
            acc_ref[...] = _dot(a_ref[...], w_ref[...])

        @pl.when((k > 0) & (k < nk - 1))
        def _():
            acc_ref[...] += _dot(a_ref[...], w_ref[...])

        @pl.when(k == nk - 1)
        def _():
            if k_last == tk:
                part = _dot(a_ref[...], w_ref[...])
            else:
                part = _dot(a_ref[:, :k_last], w_ref[:k_last, :])
            res_ref[...] = xres_ref[...] + scale * (acc_ref[...] + part)


def _mm_res(a, w, xres, scale, l=None, tm=1024, tn=2048, tk=2048):
    t, kdim = a.shape
    n = w.shape[-1]
    nk = pl.cdiv(kdim, tk)
    k_last = kdim - (nk - 1) * tk
    ni, nj = t // tm, n // tn
    ntiles = ni * nj
    nchunk = 1 << (min(nk, MM_RES_CHUNKS).bit_length() - 1)
    cr = tm // nchunk
    assert nk >= 2 and cr * nchunk == tm and cr % SUBLANES == 0
    cur = lambda tile: jnp.minimum(tile, ntiles - 1)
    prev = lambda tile: jnp.maximum(tile - 1, 0)
    k_blk = lambda tile, k: jnp.where(tile == ntiles, nk - 1, k)
    chunk = lambda k: jnp.minimum(k, nchunk - 1)
    out_chunk = lambda tile, k: jnp.where(tile == 0, 0, chunk(k))
    if l is None:
        w_spec = pl.BlockSpec((tk, tn), lambda tile, k: (k_blk(tile, k), cur(tile) % nj))
    else:
        w_spec = pl.BlockSpec((None, tk, tn), lambda tile, k: (l, k_blk(tile, k), cur(tile) % nj))
    return pl.pallas_call(
        functools.partial(_mm_res_kernel, scale=scale, nk=nk, k_last=k_last, nchunk=nchunk, ntiles=ntiles),
        grid=(ntiles + 1, nk),
        in_specs=[pl.BlockSpec((tm, tk), lambda tile, k: (cur(tile) // nj, k_blk(tile, k))),
                  w_spec,
                  pl.BlockSpec((cr, tn), lambda tile, k: ((cur(tile) // nj) * nchunk + chunk(k), cur(tile) % nj))],
        out_specs=pl.BlockSpec((cr, tn), lambda tile, k: ((prev(tile) // nj) * nchunk + out_chunk(tile, k),
                                                          prev(tile) % nj)),
        out_shape=jax.ShapeDtypeStruct((t, n), F32),
        scratch_shapes=[pltpu.VMEM((tm, tn), F32), pltpu.VMEM((tm, tn), F32), pltpu.VMEM((tm, tn), F32)],
        compiler_params=_cparams(("arbitrary", "arbitrary"), 60),
        name="mm_res",
    )(a, w, xres)


def _merge_kernel(xn_ref, y0_ref, y1_ref, y2_ref, y3_ref, wm_ref, bm_ref, w0_ref, w1_ref, w2_ref, w3_ref, o_ref,
                  wm_bf, wb_bf):
    j = pl.program_id(0)
    i = pl.program_id(1)
    sub_m, sub_b = wm_ref.shape[1], w0_ref.shape[0]
    rows_m = pl.ds(pl.multiple_of(i * sub_m, sub_m), sub_m)
    rows_b = pl.ds(pl.multiple_of(i * sub_b, sub_b), sub_b)
    for n, wb_ref in enumerate((w0_ref, w1_ref, w2_ref, w3_ref)):
        wm_bf[j % 2, n, rows_m, :] = wm_ref[n].astype(BF16)
        wb_bf[j % 2, n, rows_b, :] = wb_ref[...].astype(BF16)

    @pl.when(j > 0)
    def _():
        cur = (j + 1) % 2
        xn = xn_ref[...]
        acc = None
        for n, y_ref in enumerate((y0_ref, y1_ref, y2_ref, y3_ref)):
            gate = jax.nn.sigmoid(_dot(xn, wm_bf[cur, n]) + bm_ref[n])
            term = gate * _dot(y_ref[...], wb_bf[cur, n])
            acc = term if acc is None else acc + term
        o_ref[...] = acc.astype(o_ref.dtype)


def _merge(xn, ys, wm, bm, wbs, l, tm=512, tn=256):
    t, d = xn.shape
    nb = wm.shape[1]
    w = wbs[0].shape[1]
    nj, ni = d // tn, t // tm
    sub_m, sub_b = d // ni, w // ni
    assert sub_m * ni == d and sub_b * ni == w and sub_b % WD_SLAB == 0
    row_tile = lambda j, i: jnp.where(j == 0, 0, i)
    stage_col = lambda j: jnp.minimum(j, nj - 1)
    out_col = lambda j: jnp.maximum(j - 1, 0)
    y_spec = pl.BlockSpec((tm, w), lambda j, i: (row_tile(j, i), 0))
    wb_spec = pl.BlockSpec((None, sub_b, tn), lambda j, i: (l, i, stage_col(j)))
    return pl.pallas_call(
        _merge_kernel,
        grid=(nj + 1, ni),
        in_specs=[pl.BlockSpec((tm, d), lambda j, i: (row_tile(j, i), 0)),
                  y_spec, y_spec, y_spec, y_spec,
                  pl.BlockSpec((None, nb, sub_m, tn), lambda j, i: (l, 0, i, stage_col(j))),
                  pl.BlockSpec((None, nb, 1, tn), lambda j, i: (l, 0, 0, out_col(j))),
                  wb_spec, wb_spec, wb_spec, wb_spec],
        out_specs=pl.BlockSpec((tm, tn), lambda j, i: (row_tile(j, i), out_col(j))),
        out_shape=jax.ShapeDtypeStruct((t, d), BF16),
        scratch_shapes=[pltpu.VMEM((2, nb, d, tn), BF16), pltpu.VMEM((2, nb, w, tn), BF16)],
        compiler_params=_cparams(("arbitrary", "arbitrary"), 56),
        name="merge",
    )(xn, *ys, wm, bm, *wbs)


def _shift_rows(x, prev8, k):
    xs = pltpu.roll(x, k, 0)
    ps = pltpu.roll(prev8, k, 0)
    row = lax.broadcasted_iota(jnp.int32, prev8.shape, 0)
    head = jnp.where(row < k, ps, xs[:SUBLANES])
    return jnp.concatenate([head, xs[SUBLANES:]], axis=0)


def _conv_kernel(h_ref, b_ref, c_ref, w_ref, o_ref, carry_ref):
    @pl.when(pl.program_id(1) == 0)
    def _():
        carry_ref[...] = jnp.zeros_like(carry_ref)

    z = c_ref[...] * h_ref[...]
    prev = carry_ref[...]
    w = w_ref[...]
    y = w[0:1] * _shift_rows(z, prev, 2) + w[1:2] * _shift_rows(z, prev, 1) + w[2:3] * z
    o_ref[...] = (b_ref[...] * y).astype(o_ref.dtype)
    carry_ref[...] = z[z.shape[0] - SUBLANES:]


def _conv_mixer(proj, w, bsz, seq, cols, ts=512):
    nt = seq // ts
    col = lambda c: pl.BlockSpec((ts, MIX_W), lambda b, t, c=c: (b * nt + t, c))
    return pl.pallas_call(
        _conv_kernel,
        grid=(bsz, nt),
        in_specs=[col(cols[0]), col(cols[1]), col(cols[2]),
                  pl.BlockSpec(w.shape, lambda b, t: (0, 0))],
        out_specs=pl.BlockSpec((ts, MIX_W), lambda b, t: (b * nt + t, 0)),
        out_shape=jax.ShapeDtypeStruct((bsz * seq, MIX_W), BF16),
        scratch_shapes=[pltpu.VMEM((SUBLANES, MIX_W), F32)],
        compiler_params=_cparams(("parallel", "arbitrary"), 32),
        name="conv_mixer",
    )(proj, proj, proj, w)


def _lru_kernel(x_ref, gate_ref, cw_ref, cb_ref, wa_ref, ba_ref, wx_ref, bx_ref, lam_ref, o_ref,
                xprev_ref, hprev_ref, a_s, b_s):
    ts = x_ref.shape[0]

    @pl.when(pl.program_id(1) == 0)
    def _():
        xprev_ref[...] = jnp.zeros_like(xprev_ref)
        hprev_ref[...] = jnp.zeros_like(hprev_ref)

    xb = x_ref[...]
    prev = xprev_ref[...]
    cw = cw_ref[...]
    xc = (cw[0:1] * _shift_rows(xb, prev, 3) + cw[1:2] * _shift_rows(xb, prev, 2)
          + cw[2:3] * _shift_rows(xb, prev, 1) + cw[3:4] * xb) + cb_ref[...]
    xprev_ref[...] = xb[ts - SUBLANES:]

    xcb = xc.astype(BF16)
    ra, rx = [], []
    for c in range(MIX_W // LRU_COLS):
        blk = xcb[:, c * LRU_COLS:(c + 1) * LRU_COLS]
        ra.append(_dot(blk, wa_ref[c]))
        rx.append(_dot(blk, wx_ref[c]))
    r_t = jax.nn.sigmoid(jnp.concatenate(ra, axis=1) + ba_ref[...])
    i_t = jax.nn.sigmoid(jnp.concatenate(rx, axis=1) + bx_ref[...])
    log_a = (-LRU_C) * r_t * _softplus(-lam_ref[...])
    th = jnp.tanh(log_a)
    a_s[...] = jnp.exp(log_a)
    b_s[...] = jnp.sqrt(-2.0 * th / (1.0 - th)) * (i_t * xc)

    row = lax.broadcasted_iota(jnp.int32, (SUBLANES, MIX_W), 0)

    def body(g, hp):
        r0 = pl.multiple_of(g * SUBLANES, SUBLANES)
        a = a_s[pl.ds(r0, SUBLANES), :]
        b = b_s[pl.ds(r0, SUBLANES), :]
        for k in (1, 2, 4):
            a_sh = jnp.where(row >= k, pltpu.roll(a, k, 0), 1.0)
            b_sh = jnp.where(row >= k, pltpu.roll(b, k, 0), 0.0)
            b = a * b_sh + b
            a = a * a_sh
        h = a * hp + b
        b_s[pl.ds(r0, SUBLANES), :] = h
        return jnp.broadcast_to(h[SUBLANES - 1:SUBLANES], (SUBLANES, MIX_W))

    hprev_ref[...] = lax.fori_loop(0, ts // SUBLANES, body, hprev_ref[...])
    o_ref[...] = (b_s[...] * _gelu_tanh(gate_ref[...])).astype(o_ref.dtype)


def _lru_mixer(proj, p, bsz, seq, cols, ts=256):
    nt = seq // ts
    col = lambda c: pl.BlockSpec((ts, MIX_W), lambda b, t, c=c: (b * nt + t, c))
    full = lambda a: pl.BlockSpec(a.shape, lambda b, t, n=a.ndim: (0,) * n)
    params = (p["lru_cw"], p["lru_cb"], p["lru_wa"], p["lru_ba"], p["lru_wx"], p["lru_bx"], p["lru_lam"])
    return pl.pallas_call(
        _lru_kernel,
        grid=(bsz, nt),
        in_specs=[col(cols[0]), col(cols[1])] + [full(a) for a in params],
        out_specs=pl.BlockSpec((ts, MIX_W), lambda b, t: (b * nt + t, 0)),
        out_shape=jax.ShapeDtypeStruct((bsz * seq, MIX_W), BF16),
        scratch_shapes=[pltpu.VMEM((SUBLANES, MIX_W), F32), pltpu.VMEM((SUBLANES, MIX_W), F32),
                        pltpu.VMEM((ts, MIX_W), F32), pltpu.VMEM((ts, MIX_W), F32)],
        compiler_params=_cparams(("parallel", "arbitrary"), 32),
        name="lru_mixer",
    )(proj, proj, *params)


def _s5_kernel(u_ref, bw_ref, cw_ref, pw_ref, d_ref, wglu_ref, bglu_ref, o_ref, s_re, s_im, carry_ref):
    ts = u_ref.shape[0]
    nblk = MIX_W // S5_COLS

    @pl.when(pl.program_id(1) == 0)
    def _():
        carry_ref[...] = jnp.zeros_like(carry_ref)

    u = u_ref[...]
    ub = u.astype(BF16)
    ys = []
    for c in range(nblk):
        bu = _dot(ub[:, c * S5_COLS:(c + 1) * S5_COLS], bw_ref[c])
        s_re[...] = bu[:, :S5_LANES]
        s_im[...] = bu[:, S5_LANES:]

        def body(g, hp, c=c):
            hpr, hpi = hp
            r0 = pl.multiple_of(g * SUBLANES, SUBLANES)
            xr = s_re[pl.ds(r0, SUBLANES), :]
            xi = s_im[pl.ds(r0, SUBLANES), :]
            for lvl, k in enumerate((1, 2, 4)):
                mr = pw_ref[c, 2 * lvl]
                mi = pw_ref[c, 2 * lvl + 1]
                sr = pltpu.roll(xr, k, 0)
                si = pltpu.roll(xi, k, 0)
                xr, xi = xr + (mr * sr - mi * si), xi + (mr * si + mi * sr)
            pr = pw_ref[c, 6]
            pi = pw_ref[c, 7]
            hr = xr + (pr * hpr - pi * hpi)
            hi = xi + (pr * hpi + pi * hpr)
            s_re[pl.ds(r0, SUBLANES), :] = hr
            s_im[pl.ds(r0, SUBLANES), :] = hi
            return (jnp.broadcast_to(hr[SUBLANES - 1:SUBLANES], hr.shape),
                    jnp.broadcast_to(hi[SUBLANES - 1:SUBLANES], hi.shape))

        hpr, hpi = lax.fori_loop(0, ts // SUBLANES, body, (carry_ref[c, 0], carry_ref[c, 1]), unroll=True)
        carry_ref[c, 0] = hpr
        carry_ref[c, 1] = hpi
        hb = jnp.concatenate([s_re[...], s_im[...]], axis=1).astype(BF16)
        ys.append(_dot(hb, cw_ref[c]))
    y = jnp.concatenate(ys, axis=1) + d_ref[...] * u
    y = _gelu_tanh(y)
    z = _dot(y.astype(BF16), wglu_ref[...]) + bglu_ref[...]
    o_ref[...] = (y * jax.nn.sigmoid(z)).astype(o_ref.dtype)


def _s5_mixer(proj, p, bsz, seq, col_idx, ts=256):
    nt = seq // ts
    nblk = MIX_W // S5_COLS
    full = lambda a: pl.BlockSpec(a.shape, lambda b, t, n=a.ndim: (0,) * n)
    params = (p["s5_bw"], p["s5_cw"], p["s5_pw"], p["s5_d"], p["s5_wglu"], p["s5_bglu"])
    return pl.pallas_call(
        _s5_kernel,
        grid=(bsz, nt),
        in_specs=[pl.BlockSpec((ts, MIX_W), lambda b, t: (b * nt + t, col_idx))] + [full(a) for a in params],
        out_specs=pl.BlockSpec((ts, MIX_W), lambda b, t: (b * nt + t, 0)),
        out_shape=jax.ShapeDtypeStruct((bsz * seq, MIX_W), BF16),
        scratch_shapes=[pltpu.VMEM((ts, S5_LANES), F32), pltpu.VMEM((ts, S5_LANES), F32),
                        pltpu.VMEM((nblk, 2, SUBLANES, S5_LANES), F32)],
        compiler_params=_cparams(("parallel", "arbitrary"), 48),
        name="s5_mixer",
    )(proj, *params)


def _gla_kernel(q_ref, k_ref, v_ref, r_ref, glr_ref, wgu_ref, bg_ref, gn_ref, o_ref, state_ref):
    ts = q_ref.shape[0]
    dk = q_ref.shape[1] // GLA_HEADS
    dv = v_ref.shape[1] // GLA_HEADS
    qscale = dk ** -0.5

    @pl.when(pl.program_id(1) == 0)
    def _():
        state_ref[...] = jnp.zeros_like(state_ref)

    ri = lax.broadcasted_iota(jnp.int32, (CHUNK, CHUNK), 0)
    ci = lax.broadcasted_iota(jnp.int32, (CHUNK, CHUNK), 1)
    tri = ri >= ci
    tri_ones = jnp.where(tri, 1.0, 0.0).astype(F32)
    gn = gn_ref[...]

    for n in range(ts // CHUNK):
        rows = slice(n * CHUNK, (n + 1) * CHUNK)
        x = _dot(glr_ref[rows, :].astype(BF16), wgu_ref[...]) + bg_ref[...]
        log_a = -_softplus(-x) / GLA_TAU
        bcum = jnp.dot(tri_ones, log_a, preferred_element_type=F32, precision=lax.Precision.HIGHEST)
        b_last = bcum[CHUNK - 1:CHUNK]
        rel = bcum - bcum[CHUNK // 2 - 1:CHUNK // 2]
        e_pos = jnp.exp(rel)
        e_neg = jnp.exp(-rel)
        e_cum = jnp.exp(bcum)
        e_last = jnp.exp(b_last - bcum)
        decay = jnp.exp(b_last)
        for h in range(GLA_HEADS):
            ks = slice(h * dk, (h + 1) * dk)
            vs = slice(h * dv, (h + 1) * dv)
            qh = q_ref[rows, ks] * qscale
            kh = k_ref[rows, ks]
            vh = v_ref[rows, vs].astype(BF16)
            a_low = _dot_nt((qh * e_pos[:, ks]).astype(BF16), (kh * e_neg[:, ks]).astype(BF16))
            a_up = _dot_nt((qh * e_neg[:, ks]).astype(BF16), (kh * e_pos[:, ks]).astype(BF16))
            attn = jnp.where(tri, a_low, a_up)
            o = _dot(attn.astype(BF16), vh)
            st = state_ref[h]
            o = o + _dot_nt((qh * e_cum[:, ks]).astype(BF16), st.astype(BF16))
            kv_t = _dot_tn(vh, (kh * e_last[:, ks]).astype(BF16))
            state_ref[h] = decay[:, ks] * st + kv_t
            o = o * lax.rsqrt(jnp.mean(o * o, axis=-1, keepdims=True) + EPS) * gn
            rg = r_ref[rows, vs]
            o_ref[rows, vs] = (o * (rg * jax.nn.sigmoid(rg))).astype(o_ref.dtype)


def _gla_mixer(proj, glr, p, bsz, seq, cols, ts=256):
    nt = seq // ts
    kw = MIX_W // 2
    full = lambda a: pl.BlockSpec(a.shape, lambda b, t, n=a.ndim: (0,) * n)
    params = (p["gla_wgu"], p["gla_bg"], p["gla_norm"])
    dv = MIX_W // GLA_HEADS
    dk = kw // GLA_HEADS
    return pl.pallas_call(
        _gla_kernel,
        grid=(bsz, nt),
        in_specs=[pl.BlockSpec((ts, kw), lambda b, t: (b * nt + t, cols[0])),
                  pl.BlockSpec((ts, kw), lambda b, t: (b * nt + t, cols[1])),
                  pl.BlockSpec((ts, MIX_W), lambda b, t: (b * nt + t, cols[2])),
                  pl.BlockSpec((ts, MIX_W), lambda b, t: (b * nt + t, cols[3])),
                  pl.BlockSpec((ts, GLA_RANK_PAD), lambda b, t: (b * nt + t, 0))]
                 + [full(a) for a in params],
        out_specs=pl.BlockSpec((ts, MIX_W), lambda b, t: (b * nt + t, 0)),
        out_shape=jax.ShapeDtypeStruct((bsz * seq, MIX_W), BF16),
        scratch_shapes=[pltpu.VMEM((GLA_HEADS, dv, dk), F32)],
        compiler_params=_cparams(("parallel", "arbitrary"), 32),
        name="gla_mixer",
    )(proj, proj, proj, proj, glr, *params)


def _block_diag(blocks):
    n, g, r, c = blocks.shape
    eye = jnp.eye(g, dtype=blocks.dtype)
    return jnp.einsum('ngrc,gk->ngrkc', blocks, eye).reshape(n, g * r, g * c)


def _s5_params(lam_re, lam_im, log_dt, b_re, b_im, c_re, c_im):
    groups = lam_re.shape[0]
    nblk = groups // S5_GPC
    dt = jnp.exp(log_dt)[:, None]
    mag = jnp.exp(lam_re * dt)
    ab_re, ab_im = mag * jnp.cos(lam_im * dt), mag * jnp.sin(lam_im * dt)
    den = lam_re * lam_re + lam_im * lam_im
    nr, ni = ab_re - 1.0, ab_im
    z_re = (nr * lam_re + ni * lam_im) / den
    z_im = (ni * lam_re - nr * lam_im) / den
    bb_re = z_re[..., None] * b_re - z_im[..., None] * b_im
    bb_im = z_re[..., None] * b_im + z_im[..., None] * b_re
    to_blk = lambda a: _block_diag(jnp.swapaxes(a, 1, 2).reshape(nblk, S5_GPC, S5_GROUP, S5_STATE))
    bw = jnp.concatenate([to_blk(bb_re), to_blk(bb_im)], axis=2).astype(BF16)
    from_blk = lambda a: _block_diag(jnp.swapaxes(a, 1, 2).reshape(nblk, S5_GPC, S5_STATE, S5_GROUP))
    cw = jnp.concatenate([from_blk(c_re), from_blk(-c_im)], axis=1).astype(BF16)
    ar = ab_re.reshape(nblk, S5_LANES)
    ai = ab_im.reshape(nblk, S5_LANES)
    pows = [(ar, ai)]
    for _ in range(SUBLANES - 1):
        pr, pi = pows[-1]
        pows.append((pr * ar - pi * ai, pr * ai + pi * ar))
    row = jnp.arange(SUBLANES)[None, :, None]
    planes = []
    for k in (1, 2, 4):
        for comp in (0, 1):
            planes.append(jnp.where(row >= k, pows[k - 1][comp][:, None, :], 0.0))
    planes.append(jnp.stack([pw[0] for pw in pows], axis=1))
    planes.append(jnp.stack([pw[1] for pw in pows], axis=1))
    pw = jnp.stack(planes, axis=1).astype(F32)
    return bw, cw, pw


def _mixer_params(l, a):
    row = lambda v: v.reshape(1, -1).astype(F32)
    p = {}
    p["s5_bw"], p["s5_cw"], p["s5_pw"] = _s5_params(
        a["s5_lam_re"][l], a["s5_lam_im"][l], a["s5_log_dt"][l], a["s5_b_re"][l], a["s5_b_im"][l],
        a["s5_c_re"][l], a["s5_c_im"][l])
    p["s5_d"] = row(a["s5_d"][l])
    p["s5_wglu"] = a["s5_w_glu"][l].astype(BF16)
    p["s5_bglu"] = row(a["s5_b_glu"][l])
    p["gla_wgu"] = jnp.pad(a["gla_w_gate_up"][l].astype(BF16), ((0, GLA_RANK_PAD - GLA_RANK), (0, 0)))
    p["gla_bg"] = row(a["gla_b_gate"][l])
    p["gla_norm"] = row(a["gla_norm"][l])
    p["conv_w"] = a["conv_w"][l].astype(F32)
    p["lru_cw"] = a["lru_conv_w"][l].astype(F32)
    p["lru_cb"] = row(a["lru_conv_b"][l])
    per_blk = LRU_COLS // LRU_BLOCK
    bd = lambda w: _block_diag(w.reshape(-1, per_blk, LRU_BLOCK, LRU_BLOCK)).astype(BF16)
    p["lru_wa"] = bd(a["lru_w_a"][l])
    p["lru_wx"] = bd(a["lru_w_x"][l])
    p["lru_ba"] = row(a["lru_b_a"][l])
    p["lru_bx"] = row(a["lru_b_x"][l])
    p["lru_lam"] = row(a["lru_lam"][l])
    return p


_COL_S5 = 0
_COLS_GLA = (2, 3, 2, 3)
_COLS_CONV = (4, 5, 6)
_COLS_LRU = (7, 8)


def kernel(x, norm_ffn1, ffn1_w_gate, ffn1_w_up, ffn1_w_down, norm_mix, w_in, s5_lam_re, s5_lam_im, s5_log_dt, s5_b_re, s5_b_im, s5_c_re, s5_c_im, s5_d, s5_w_glu, s5_b_glu, gla_w_gate_up, gla_b_gate, gla_norm, conv_w, lru_conv_w, lru_conv_b, lru_w_a, lru_b_a, lru_w_x, lru_b_x, lru_lam, w_br_s5, w_br_gla, w_br_conv, w_br_lru, w_merge, b_merge, w_out, norm_ffn2, ffn2_w_gate, ffn2_w_up, ffn2_w_down, norm_final):
    a = dict(locals())
    bsz, seq, d = x.shape
    depth = norm_ffn1.shape[0]
    glr0 = MIX_W + MIX_W // 2 + MIX_W // 2 + MIX_W
    w_main = jnp.concatenate([w_in[:, :, :glr0], w_in[:, :, glr0 + GLA_RANK:]], axis=2).astype(BF16)
    w_glr = jnp.pad(w_in[:, :, glr0:glr0 + GLA_RANK].astype(BF16), ((0, 0), (0, 0), (0, GLA_RANK_PAD - GLA_RANK)))
    b_merge_r = b_merge.reshape(depth, -1, 1, d)
    w_brs = (w_br_s5, w_br_gla, w_br_conv, w_br_lru)
    w_out_bf = w_out.astype(BF16)

    xf = x.reshape(bsz * seq, d)
    for l in range(depth):
        p = _mixer_params(l, a)
        h = _rmsnorm(xf, norm_ffn1[l], BF16)
        act, wd = _ffn_up(h, ffn1_w_gate, ffn1_w_up, ffn1_w_down, l)
        xf = _mm_res(act, wd, xf, FFN_RES)

        xn = _rmsnorm(xf, norm_mix[l], BF16)
        proj = _matmul(xn, w_main, l, F32, tm=1024, tn=1024)
        glr = _matmul(xn, w_glr, l, F32, tm=1024, tn=GLA_RANK_PAD)
        y_s5 = _s5_mixer(proj, p, bsz, seq, _COL_S5)
        y_gla = _gla_mixer(proj, glr, p, bsz, seq, _COLS_GLA)
        y_conv = _conv_mixer(proj, p["conv_w"], bsz, seq, _COLS_CONV)
        y_lru = _lru_mixer(proj, p, bsz, seq, _COLS_LRU)
        merged = _merge(xn, (y_s5, y_gla, y_conv, y_lru), w_merge, b_merge_r, w_brs, l)
        xf = _matmul_res(merged, w_out_bf, l, xf)

        h = _rmsnorm(xf, norm_ffn2[l], BF16)
        act, wd = _ffn_up(h, ffn2_w_gate, ffn2_w_up, ffn2_w_down, l)
        xf = _mm_res(act, wd, xf, FFN_RES)
    return _rmsnorm(xf, norm_final, F32).reshape(bsz, seq, d)
```

```python
import functools
import math

import jax
import jax.numpy as jnp
from jax import lax
from jax.experimental import pallas as pl
from jax.experimental.pallas import tpu as pltpu

F32 = jnp.float32
BF16 = jnp.bfloat16

EPS = 1e-6
CHUNK = 64
FFN_RES = 0.5
SUBLANES = 8
MIB = 2 ** 20
EPI_ROWS = 16
NORM_UNROLL = 4
MM_RES_CHUNKS = 8
WD_SLAB = 16

S5_GROUP = 16
S5_STATE = 64
S5_COLS = 256
S5_GPC = S5_COLS // S5_GROUP
S5_LANES = S5_GPC * S5_STATE
GLA_HEADS = 4
GLA_RANK = 16
GLA_RANK_PAD = 128
GLA_TAU = 16.0
LRU_BLOCK = 64
LRU_C = 8.0
LRU_COLS = 256
MIX_W = 1024


def _cparams(sem, vmem_mib):
    return pltpu.CompilerParams(dimension_semantics=sem, vmem_limit_bytes=vmem_mib * MIB)


def _dot(a, b):
    return jnp.dot(a, b, preferred_element_type=F32)


def _dot_nt(a, b):
    return lax.dot_general(a, b, (((1,), (1,)), ((), ())), preferred_element_type=F32)


def _dot_tn(a, b):
    return lax.dot_general(a, b, (((0,), (0,)), ((), ())), preferred_element_type=F32)


def _softplus(x):
    return jnp.maximum(x, 0.0) + jnp.log1p(jnp.exp(-jnp.abs(x)))


def _gelu_tanh(x):
    return 0.5 * x * (1.0 + jnp.tanh(math.sqrt(2.0 / math.pi) * (x + 0.044715 * (x * x * x))))


def _rms_scale(x, g):
    ms = jnp.mean(x * x, axis=-1, keepdims=True)
    return x * lax.rsqrt(ms + EPS) * g


def _rmsnorm_kernel(x_ref, g_ref, o_ref):
    g = g_ref[...]

    def rows_body(r, carry):
        rows = pl.ds(pl.multiple_of(r * EPI_ROWS, EPI_ROWS), EPI_ROWS)
        o_ref[rows, :] = _rms_scale(x_ref[rows, :], g).astype(o_ref.dtype)
        return carry

    lax.fori_loop(0, x_ref.shape[0] // EPI_ROWS, rows_body, 0, unroll=NORM_UNROLL)


def _rmsnorm(x, g, out_dtype, tm=512):
    t, d = x.shape
    return pl.pallas_call(
        _rmsnorm_kernel,
        grid=(t // tm,),
        in_specs=[pl.BlockSpec((tm, d), lambda i: (i, 0)),
                  pl.BlockSpec((1, d), lambda i: (0, 0))],
        out_specs=pl.BlockSpec((tm, d), lambda i: (i, 0)),
        out_shape=jax.ShapeDtypeStruct((t, d), out_dtype),
        compiler_params=_cparams(("parallel",), 40),
        name="rmsnorm",
    )(x, g.reshape(1, d))


def _permute_cast_kernel(w_ref, main_ref, glr_ref, *, glr0):
    w = w_ref[...]
    main_ref[:, :glr0] = w[:, :glr0].astype(BF16)
    main_ref[:, glr0:] = w[:, glr0 + GLA_RANK:].astype(BF16)
    g = w[:, glr0:glr0 + GLA_RANK_PAD]
    lane = lax.broadcasted_iota(jnp.int32, g.shape, 1)
    glr_ref[...] = jnp.where(lane < GLA_RANK, g, 0.0).astype(BF16)


def _permute_cast(w_in, glr0, rows=256):
    depth, d, n = w_in.shape
    blk = lambda width: pl.BlockSpec((None, rows, width), lambda l, i: (l, i, 0))
    return pl.pallas_call(
        functools.partial(_permute_cast_kernel, glr0=glr0),
        grid=(depth, d // rows),
        in_specs=[blk(n)],
        out_specs=[blk(n - GLA_RANK), blk(GLA_RANK_PAD)],
        out_shape=[jax.ShapeDtypeStruct((depth, d, n - GLA_RANK), BF16),
                   jax.ShapeDtypeStruct((depth, d, GLA_RANK_PAD), BF16)],
        compiler_params=_cparams(("parallel", "parallel"), 48),
        name="permute_cast",
    )(w_in)


def _matmul_kernel(x_ref, w_ref, o_ref):
    o_ref[...] = _dot(x_ref[...], w_ref[...]).astype(o_ref.dtype)


def _matmul(x, w, l, out_dtype, tm, tn):
    t, k = x.shape
    n = w.shape[2]
    return pl.pallas_call(
        _matmul_kernel,
        grid=(t // tm, n // tn),
        in_specs=[pl.BlockSpec((tm, k), lambda i, j: (i, 0)),
                  pl.BlockSpec((None, k, tn), lambda i, j: (l, 0, j))],
        out_specs=pl.BlockSpec((tm, tn), lambda i, j: (i, j)),
        out_shape=jax.ShapeDtypeStruct((t, n), out_dtype),
        compiler_params=_cparams(("parallel", "arbitrary"), 48),
        name="proj_matmul",
    )(x, w)


def _matmul_res_kernel(x_ref, w_ref, r_ref, o_ref):
    o_ref[...] = r_ref[...] + _dot(x_ref[...], w_ref[...])


def _matmul_res(x, w, l, res, tm=1024, tn=1024):
    t, k = x.shape
    n = w.shape[2]
    return pl.pallas_call(
        _matmul_res_kernel,
        grid=(t // tm, n // tn),
        in_specs=[pl.BlockSpec((tm, k), lambda i, j: (i, 0)),
                  pl.BlockSpec((None, k, tn), lambda i, j: (l, 0, j)),
                  pl.BlockSpec((tm, tn), lambda i, j: (i, j))],
        out_specs=pl.BlockSpec((tm, tn), lambda i, j: (i, j)),
        out_shape=jax.ShapeDtypeStruct((t, n), F32),
        compiler_params=_cparams(("parallel", "arbitrary"), 56),
        name="matmul_res",
    )(x, w, res)


def _ffn_up_kernel(h_ref, wg_ref, wu_ref, wd_ref, o_ref, wdb_ref, wgu_bf, *, nj, n_last):
    j = pl.program_id(0)
    i = pl.program_id(1)
    sub, tn = wg_ref.shape
    rows = pl.ds(pl.multiple_of(i * sub, sub), sub)
    wgu_bf[j % 2, rows, :tn] = wg_ref[...].astype(BF16)
    wgu_bf[j % 2, rows, tn:] = wu_ref[...].astype(BF16)
    wdb_ref[...] = wd_ref[...].astype(BF16)
    cur = (j + 1) % 2

    @pl.when((j > 0) & (j < nj))
    def _():
        gu = _dot(h_ref[...], wgu_bf[cur])
        g = gu[:, :tn]
        o_ref[...] = (g * jax.nn.sigmoid(g) * gu[:, tn:]).astype(o_ref.dtype)

    @pl.when(j == nj)
    def _():
        h = h_ref[...]
        g = _dot(h, wgu_bf[cur, :, :n_last])
        u = _dot(h, wgu_bf[cur, :, tn:tn + n_last])
        o_ref[:, :n_last] = (g * jax.nn.sigmoid(g) * u).astype(o_ref.dtype)


def _ffn_up(h, wg, wu, wd, l, tm=1024, tn=512):
    t, d = h.shape
    f = wg.shape[2]
    nj, ni = pl.cdiv(f, tn), t // tm
    sub = d // ni
    assert sub * ni == d and sub % WD_SLAB == 0
    steps = (nj + 1) * ni
    slab_rows = next(r for r in range(WD_SLAB, f + 1, WD_SLAB) if f % r == 0 and f // r <= steps)
    slabs = f // slab_rows
    slab = lambda j, i: jnp.minimum(j * ni + i, slabs - 1)
    row_tile = lambda j, i: jnp.where(j == 0, 0, i)
    return pl.pallas_call(
        functools.partial(_ffn_up_kernel, nj=nj, n_last=f - (nj - 1) * tn),
        grid=(nj + 1, ni),
        in_specs=[pl.BlockSpec((tm, d), lambda j, i: (row_tile(j, i), 0)),
                  pl.BlockSpec((None, sub, tn), lambda j, i: (l, i, jnp.minimum(j, nj - 1))),
                  pl.BlockSpec((None, sub, tn), lambda j, i: (l, i, jnp.minimum(j, nj - 1))),
                  pl.BlockSpec((None, slab_rows, d), lambda j, i: (l, slab(j, i), 0))],
        out_specs=[pl.BlockSpec((tm, tn), lambda j, i: (row_tile(j, i), jnp.maximum(j - 1, 0))),
                   pl.BlockSpec((slab_rows, d), lambda j, i: (slab(j, i), 0))],
        out_shape=[jax.ShapeDtypeStruct((t, f), BF16),
                   jax.ShapeDtypeStruct((f, d), BF16)],
        scratch_shapes=[pltpu.VMEM((2, d, 2 * tn), BF16)],
        compiler_params=_cparams(("arbitrary", "arbitrary"), 56),
        name="ffn_up",
    )(h, wg, wu, wd)


def _mm_res_kernel(a_ref, w_ref, xs_ref, o_ref, acc_ref, xres_ref, res_ref, *, scale, nk, k_last, nchunk, ntiles):
    tile = pl.program_id(0)
    k = pl.program_id(1)
    tk = a_ref.shape[1]
    cr = xs_ref.shape[0]
    kc = jnp.minimum(k, nchunk - 1)
    rows = pl.ds(pl.multiple_of(kc * cr, cr), cr)

    @pl.when(tile > 0)
    def _():
        o_ref[...] = res_ref[rows, :]

    @pl.when(tile < ntiles)
    def _():
        xres_ref[rows, :] = xs_ref[...]

        @pl.when(k == 0)
        def _():
            acc_ref[...] = _dot(a_ref[...], w_ref[...])

        @pl.when((k > 0) & (k < nk - 1))
        def _():
            acc_ref[...] += _dot(a_ref[...], w_ref[...])

        @pl.when(k == nk - 1)
        def _():
            if k_last == tk:
                part = _dot(a_ref[...], w_ref[...])
            else:
                part = _dot(a_ref[:, :k_last], w_ref[:k_last, :])
            res_ref[...] = xres_ref[...] + scale * (acc_ref[...] + part)


def _mm_res(a, w, xres, scale, l=None, tm=1024, tn=2048, tk=1024):
    t, kdim = a.shape
    n = w.shape[-1]
    nk = pl.cdiv(kdim, tk)
    k_last = kdim - (nk - 1) * tk
    ni, nj = t // tm, n // tn
    ntiles = ni * nj
    nchunk = 1 << (min(nk, MM_RES_CHUNKS).bit_length() - 1)
    cr = tm // nchunk
    assert nk >= 2 and cr * nchunk == tm and cr % SUBLANES == 0
    cur = lambda tile: jnp.minimum(tile, ntiles - 1)
    prev = lambda tile: jnp.maximum(tile - 1, 0)
    k_blk = lambda tile, k: jnp.where(tile == ntiles, nk - 1, k)
    chunk = lambda k: jnp.minimum(k, nchunk - 1)
    out_chunk = lambda tile, k: jnp.where(tile == 0, 0, chunk(k))
    if l is None:
        w_spec = pl.BlockSpec((tk, tn), lambda tile, k: (k_blk(tile, k), cur(tile) % nj))
    else:
        w_spec = pl.BlockSpec((None, tk, tn), lambda tile, k: (l, k_blk(tile, k), cur(tile) % nj))
    return pl.pallas_call(
        functools.partial(_mm_res_kernel, scale=scale, nk=nk, k_last=k_last, nchunk=nchunk, ntiles=ntiles),
        grid=(ntiles + 1, nk),
        in_specs=[pl.BlockSpec((tm, tk), lambda tile, k: (cur(tile) // nj, k_blk(tile, k))),
                  w_spec,
                  pl.BlockSpec((cr, tn), lambda tile, k: ((cur(tile) // nj) * nchunk + chunk(k), cur(tile) % nj))],
        out_specs=pl.BlockSpec((cr, tn), lambda tile, k: ((prev(tile) // nj) * nchunk + out_chunk(tile, k),
                                                          prev(tile) % nj)),
        out_shape=jax.ShapeDtypeStruct((t, n), F32),
        scratch_shapes=[pltpu.VMEM((tm, tn), F32), pltpu.VMEM((tm, tn), F32), pltpu.VMEM((tm, tn), F32)],
        compiler_params=_cparams(("arbitrary", "arbitrary"), 60),
        name="mm_res",
    )(a, w, xres)


def _merge_kernel(xn_ref, y0_ref, y1_ref, y2_ref, y3_ref, wm_ref, bm_ref, w0_ref, w1_ref, w2_ref, w3_ref, o_ref,
                  wm_bf, wb_bf):
    j = pl.program_id(0)
    i = pl.program_id(1)
    sub_m, sub_b = wm_ref.shape[1], w0_ref.shape[0]
    tn = o_ref.shape[1]
    rows_m = pl.ds(pl.multiple_of(i * sub_m, sub_m), sub_m)
    rows_b = pl.ds(pl.multiple_of(i * sub_b, sub_b), sub_b)
    for n, wb_ref in enumerate((w0_ref, w1_ref, w2_ref, w3_ref)):
        wm_bf[j % 2, rows_m, n * tn:(n + 1) * tn] = wm_ref[n].astype(BF16)
        wb_bf[j % 2, n, rows_b, :] = wb_ref[...].astype(BF16)

    @pl.when(j > 0)
    def _():
        cur = (j + 1) % 2
        pre = _dot(xn_ref[...], wm_bf[cur])
        acc = None
        for n, y_ref in enumerate((y0_ref, y1_ref, y2_ref, y3_ref)):
            gate = jax.nn.sigmoid(pre[:, n * tn:(n + 1) * tn] + bm_ref[n])
            term = gate * _dot(y_ref[...], wb_bf[cur, n])
            acc = term if acc is None else acc + term
        o_ref[...] = acc.astype(o_ref.dtype)


def _merge(xn, ys, wm, bm, wbs, l, tm=512, tn=256):
    t, d = xn.shape
    nb = wm.shape[1]
    w = wbs[0].shape[1]
    nj, ni = d // tn, t // tm
    sub_m, sub_b = d // ni, w // ni
    assert sub_m * ni == d and sub_b * ni == w and sub_b % WD_SLAB == 0
    row_tile = lambda j, i: jnp.where(j == 0, 0, i)
    stage_col = lambda j: jnp.minimum(j, nj - 1)
    out_col = lambda j: jnp.maximum(j - 1, 0)
    y_spec = pl.BlockSpec((tm, w), lambda j, i: (row_tile(j, i), 0))
    wb_spec = pl.BlockSpec((None, sub_b, tn), lambda j, i: (l, i, stage_col(j)))
    return pl.pallas_call(
        _merge_kernel,
        grid=(nj + 1, ni),
        in_specs=[pl.BlockSpec((tm, d), lambda j, i: (row_tile(j, i), 0)),
                  y_spec, y_spec, y_spec, y_spec,
                  pl.BlockSpec((None, nb, sub_m, tn), lambda j, i: (l, 0, i, stage_col(j))),
                  pl.BlockSpec((None, nb, 1, tn), lambda j, i: (l, 0, 0, out_col(j))),
                  wb_spec, wb_spec, wb_spec, wb_spec],
        out_specs=pl.BlockSpec((tm, tn), lambda j, i: (row_tile(j, i), out_col(j))),
        out_shape=jax.ShapeDtypeStruct((t, d), BF16),
        scratch_shapes=[pltpu.VMEM((2, d, nb * tn), BF16), pltpu.VMEM((2, nb, w, tn), BF16)],
        compiler_params=_cparams(("arbitrary", "arbitrary"), 56),
        name="merge",
    )(xn, *ys, wm, bm, *wbs)


def _shift_rows(x, prev8, k):
    xs = pltpu.roll(x, k, 0)
    ps = pltpu.roll(prev8, k, 0)
    row = lax.broadcasted_iota(jnp.int32, prev8.shape, 0)
    head = jnp.where(row < k, ps, xs[:SUBLANES])
    return jnp.concatenate([head, xs[SUBLANES:]], axis=0)


def _conv_kernel(h_ref, b_ref, c_ref, w_ref, o_ref, carry_ref):
    @pl.when(pl.program_id(1) == 0)
    def _():
        carry_ref[...] = jnp.zeros_like(carry_ref)

    z = c_ref[...] * h_ref[...]
    prev = carry_ref[...]
    w = w_ref[...]
    y = w[0:1] * _shift_rows(z, prev, 2) + w[1:2] * _shift_rows(z, prev, 1) + w[2:3] * z
    o_ref[...] = (b_ref[...] * y).astype(o_ref.dtype)
    carry_ref[...] = z[z.shape[0] - SUBLANES:]


def _conv_mixer(proj, w, bsz, seq, cols, ts=512):
    nt = seq // ts
    col = lambda c: pl.BlockSpec((ts, MIX_W), lambda b, t, c=c: (b * nt + t, c))
    return pl.pallas_call(
        _conv_kernel,
        grid=(bsz, nt),
        in_specs=[col(cols[0]), col(cols[1]), col(cols[2]),
                  pl.BlockSpec(w.shape, lambda b, t: (0, 0))],
        out_specs=pl.BlockSpec((ts, MIX_W), lambda b, t: (b * nt + t, 0)),
        out_shape=jax.ShapeDtypeStruct((bsz * seq, MIX_W), BF16),
        scratch_shapes=[pltpu.VMEM((SUBLANES, MIX_W), F32)],
        compiler_params=_cparams(("parallel", "arbitrary"), 32),
        name="conv_mixer",
    )(proj, proj, proj, w)


def _lru_kernel(x_ref, gate_ref, cw_ref, cb_ref, wa_ref, ba_ref, wx_ref, bx_ref, lam_ref, o_ref,
                xprev_ref, hprev_ref, a_s, b_s):
    ts = x_ref.shape[0]

    @pl.when(pl.program_id(1) == 0)
    def _():
        xprev_ref[...] = jnp.zeros_like(xprev_ref)
        hprev_ref[...] = jnp.zeros_like(hprev_ref)

    xb = x_ref[...]
    prev = xprev_ref[...]
    cw = cw_ref[...]
    xc = (cw[0:1] * _shift_rows(xb, prev, 3) + cw[1:2] * _shift_rows(xb, prev, 2)
          + cw[2:3] * _shift_rows(xb, prev, 1) + cw[3:4] * xb) + cb_ref[...]
    xprev_ref[...] = xb[ts - SUBLANES:]

    xcb = xc.astype(BF16)
    ra, rx = [], []
    for c in range(MIX_W // LRU_COLS):
        blk = xcb[:, c * LRU_COLS:(c + 1) * LRU_COLS]
        ra.append(_dot(blk, wa_ref[c]))
        rx.append(_dot(blk, wx_ref[c]))
    r_t = jax.nn.sigmoid(jnp.concatenate(ra, axis=1) + ba_ref[...])
    i_t = jax.nn.sigmoid(jnp.concatenate(rx, axis=1) + bx_ref[...])
    log_a = (-LRU_C) * r_t * _softplus(-lam_ref[...])
    th = jnp.tanh(log_a)
    a_s[...] = jnp.exp(log_a)
    b_s[...] = jnp.sqrt(-2.0 * th / (1.0 - th)) * (i_t * xc)

    row = lax.broadcasted_iota(jnp.int32, (SUBLANES, MIX_W), 0)

    def body(g, hp):
        r0 = pl.multiple_of(g * SUBLANES, SUBLANES)
        a = a_s[pl.ds(r0, SUBLANES), :]
        b = b_s[pl.ds(r0, SUBLANES), :]
        for k in (1, 2, 4):
            a_sh = jnp.where(row >= k, pltpu.roll(a, k, 0), 1.0)
            b_sh = jnp.where(row >= k, pltpu.roll(b, k, 0), 0.0)
            b = a * b_sh + b
            a = a * a_sh
        h = a * hp + b
        b_s[pl.ds(r0, SUBLANES), :] = h
        return jnp.broadcast_to(h[SUBLANES - 1:SUBLANES], (SUBLANES, MIX_W))

    hprev_ref[...] = lax.fori_loop(0, ts // SUBLANES, body, hprev_ref[...], unroll=True)
    o_ref[...] = (b_s[...] * _gelu_tanh(gate_ref[...])).astype(o_ref.dtype)


def _lru_mixer(proj, p, bsz, seq, cols, ts=256):
    nt = seq // ts
    col = lambda c: pl.BlockSpec((ts, MIX_W), lambda b, t, c=c: (b * nt + t, c))
    full = lambda a: pl.BlockSpec(a.shape, lambda b, t, n=a.ndim: (0,) * n)
    params = (p["lru_cw"], p["lru_cb"], p["lru_wa"], p["lru_ba"], p["lru_wx"], p["lru_bx"], p["lru_lam"])
    return pl.pallas_call(
        _lru_kernel,
        grid=(bsz, nt),
        in_specs=[col(cols[0]), col(cols[1])] + [full(a) for a in params],
        out_specs=pl.BlockSpec((ts, MIX_W), lambda b, t: (b * nt + t, 0)),
        out_shape=jax.ShapeDtypeStruct((bsz * seq, MIX_W), BF16),
        scratch_shapes=[pltpu.VMEM((SUBLANES, MIX_W), F32), pltpu.VMEM((SUBLANES, MIX_W), F32),
                        pltpu.VMEM((ts, MIX_W), F32), pltpu.VMEM((ts, MIX_W), F32)],
        compiler_params=_cparams(("parallel", "arbitrary"), 32),
        name="lru_mixer",
    )(proj, proj, *params)


def _s5_kernel(u_ref, bw_ref, cw_ref, pw_ref, d_ref, wglu_ref, bglu_ref, o_ref, s_re, s_im, carry_ref):
    ts = u_ref.shape[0]
    nblk = MIX_W // S5_COLS

    @pl.when(pl.program_id(1) == 0)
    def _():
        carry_ref[...] = jnp.zeros_like(carry_ref)

    u = u_ref[...]
    ub = u.astype(BF16)
    ys = []
    for c in range(nblk):
        bu = _dot(ub[:, c * S5_COLS:(c + 1) * S5_COLS], bw_ref[c])
        s_re[...] = bu[:, :S5_LANES]
        s_im[...] = bu[:, S5_LANES:]

        def body(g, hp, c=c):
            hpr, hpi = hp
            r0 = pl.multiple_of(g * SUBLANES, SUBLANES)
            xr = s_re[pl.ds(r0, SUBLANES), :]
            xi = s_im[pl.ds(r0, SUBLANES), :]
            for lvl, k in enumerate((1, 2, 4)):
                mr = pw_ref[c, 2 * lvl]
                mi = pw_ref[c, 2 * lvl + 1]
                sr = pltpu.roll(xr, k, 0)
                si = pltpu.roll(xi, k, 0)
                xr, xi = xr + (mr * sr - mi * si), xi + (mr * si + mi * sr)
            pr = pw_ref[c, 6]
            pi = pw_ref[c, 7]
            hr = xr + (pr * hpr - pi * hpi)
            hi = xi + (pr * hpi + pi * hpr)
            s_re[pl.ds(r0, SUBLANES), :] = hr
            s_im[pl.ds(r0, SUBLANES), :] = hi
            return (jnp.broadcast_to(hr[SUBLANES - 1:SUBLANES], hr.shape),
                    jnp.broadcast_to(hi[SUBLANES - 1:SUBLANES], hi.shape))

        hpr, hpi = lax.fori_loop(0, ts // SUBLANES, body, (carry_ref[c, 0], carry_ref[c, 1]), unroll=True)
        carry_ref[c, 0] = hpr
        carry_ref[c, 1] = hpi
        hb = jnp.concatenate([s_re[...], s_im[...]], axis=1).astype(BF16)
        ys.append(_dot(hb, cw_ref[c]))
    y = jnp.concatenate(ys, axis=1) + d_ref[...] * u
    y = _gelu_tanh(y)
    z = _dot(y.astype(BF16), wglu_ref[...]) + bglu_ref[...]
    o_ref[...] = (y * jax.nn.sigmoid(z)).astype(o_ref.dtype)


def _s5_mixer(proj, p, bsz, seq, col_idx, ts=256):
    nt = seq // ts
    nblk = MIX_W // S5_COLS
    full = lambda a: pl.BlockSpec(a.shape, lambda b, t, n=a.ndim: (0,) * n)
    params = (p["s5_bw"], p["s5_cw"], p["s5_pw"], p["s5_d"], p["s5_wglu"], p["s5_bglu"])
    return pl.pallas_call(
        _s5_kernel,
        grid=(bsz, nt),
        in_specs=[pl.BlockSpec((ts, MIX_W), lambda b, t: (b * nt + t, col_idx))] + [full(a) for a in params],
        out_specs=pl.BlockSpec((ts, MIX_W), lambda b, t: (b * nt + t, 0)),
        out_shape=jax.ShapeDtypeStruct((bsz * seq, MIX_W), BF16),
        scratch_shapes=[pltpu.VMEM((ts, S5_LANES), F32), pltpu.VMEM((ts, S5_LANES), F32),
                        pltpu.VMEM((nblk, 2, SUBLANES, S5_LANES), F32)],
        compiler_params=_cparams(("parallel", "arbitrary"), 48),
        name="s5_mixer",
    )(proj, *params)


def _gla_kernel(q_ref, k_ref, v_ref, r_ref, glr_ref, wgu_ref, bg_ref, gn_ref, o_ref, state_ref):
    ts = q_ref.shape[0]
    dk = q_ref.shape[1] // GLA_HEADS
    dv = v_ref.shape[1] // GLA_HEADS
    qscale = dk ** -0.5

    @pl.when(pl.program_id(1) == 0)
    def _():
        state_ref[...] = jnp.zeros_like(state_ref)

    ri = lax.broadcasted_iota(jnp.int32, (CHUNK, CHUNK), 0)
    ci = lax.broadcasted_iota(jnp.int32, (CHUNK, CHUNK), 1)
    tri = ri >= ci
    tri_ones = jnp.where(tri, 1.0, 0.0).astype(F32)
    gn = gn_ref[...]

    for n in range(ts // CHUNK):
        rows = slice(n * CHUNK, (n + 1) * CHUNK)
        x = _dot(glr_ref[rows, :].astype(BF16), wgu_ref[...]) + bg_ref[...]
        log_a = -_softplus(-x) / GLA_TAU
        bcum = jnp.dot(tri_ones, log_a, preferred_element_type=F32, precision=lax.Precision.HIGHEST)
        b_last = bcum[CHUNK - 1:CHUNK]
        rel = bcum - bcum[CHUNK // 2 - 1:CHUNK // 2]
        e_pos = jnp.exp(rel)
        e_neg = jnp.exp(-rel)
        e_cum = jnp.exp(bcum)
        e_last = jnp.exp(b_last - bcum)
        decay = jnp.exp(b_last)
        for h in range(GLA_HEADS):
            ks = slice(h * dk, (h + 1) * dk)
            vs = slice(h * dv, (h + 1) * dv)
            qh = q_ref[rows, ks] * qscale
            kh = k_ref[rows, ks]
            vh = v_ref[rows, vs].astype(BF16)
            a_low = _dot_nt((qh * e_pos[:, ks]).astype(BF16), (kh * e_neg[:, ks]).astype(BF16))
            a_up = _dot_nt((qh * e_neg[:, ks]).astype(BF16), (kh * e_pos[:, ks]).astype(BF16))
            attn = jnp.where(tri, a_low, a_up)
            o = _dot(attn.astype(BF16), vh)
            st = state_ref[h]
            o = o + _dot_nt((qh * e_cum[:, ks]).astype(BF16), st.astype(BF16))
            kv_t = _dot_tn(vh, (kh * e_last[:, ks]).astype(BF16))
            state_ref[h] = decay[:, ks] * st + kv_t
            o = o * lax.rsqrt(jnp.mean(o * o, axis=-1, keepdims=True) + EPS) * gn
            rg = r_ref[rows, vs]
            o_ref[rows, vs] = (o * (rg * jax.nn.sigmoid(rg))).astype(o_ref.dtype)


def _gla_mixer(proj, glr, p, bsz, seq, cols, ts=256):
    nt = seq // ts
    kw = MIX_W // 2
    full = lambda a: pl.BlockSpec(a.shape, lambda b, t, n=a.ndim: (0,) * n)
    params = (p["gla_wgu"], p["gla_bg"], p["gla_norm"])
    dv = MIX_W // GLA_HEADS
    dk = kw // GLA_HEADS
    return pl.pallas_call(
        _gla_kernel,
        grid=(bsz, nt),
        in_specs=[pl.BlockSpec((ts, kw), lambda b, t: (b * nt + t, cols[0])),
                  pl.BlockSpec((ts, kw), lambda b, t: (b * nt + t, cols[1])),
                  pl.BlockSpec((ts, MIX_W), lambda b, t: (b * nt + t, cols[2])),
                  pl.BlockSpec((ts, MIX_W), lambda b, t: (b * nt + t, cols[3])),
                  pl.BlockSpec((ts, GLA_RANK_PAD), lambda b, t: (b * nt + t, 0))]
                 + [full(a) for a in params],
        out_specs=pl.BlockSpec((ts, MIX_W), lambda b, t: (b * nt + t, 0)),
        out_shape=jax.ShapeDtypeStruct((bsz * seq, MIX_W), BF16),
        scratch_shapes=[pltpu.VMEM((GLA_HEADS, dv, dk), F32)],
        compiler_params=_cparams(("parallel", "arbitrary"), 32),
        name="gla_mixer",
    )(proj, proj, proj, proj, glr, *params)


def _block_diag(blocks):
    n, g, r, c = blocks.shape
    eye = jnp.eye(g, dtype=blocks.dtype)
    return jnp.einsum('ngrc,gk->ngrkc', blocks, eye).reshape(n, g * r, g * c)


def _s5_params(lam_re, lam_im, log_dt, b_re, b_im, c_re, c_im):
    groups = lam_re.shape[0]
    nblk = groups // S5_GPC
    dt = jnp.exp(log_dt)[:, None]
    mag = jnp.exp(lam_re * dt)
    ab_re, ab_im = mag * jnp.cos(lam_im * dt), mag * jnp.sin(lam_im * dt)
    den = lam_re * lam_re + lam_im * lam_im
    nr, ni = ab_re - 1.0, ab_im
    z_re = (nr * lam_re + ni * lam_im) / den
    z_im = (ni * lam_re - nr * lam_im) / den
    bb_re = z_re[..., None] * b_re - z_im[..., None] * b_im
    bb_im = z_re[..., None] * b_im + z_im[..., None] * b_re
    to_blk = lambda a: _block_diag(jnp.swapaxes(a, 1, 2).reshape(nblk, S5_GPC, S5_GROUP, S5_STATE))
    bw = jnp.concatenate([to_blk(bb_re), to_blk(bb_im)], axis=2).astype(BF16)
    from_blk = lambda a: _block_diag(jnp.swapaxes(a, 1, 2).reshape(nblk, S5_GPC, S5_STATE, S5_GROUP))
    cw = jnp.concatenate([from_blk(c_re), from_blk(-c_im)], axis=1).astype(BF16)
    ar = ab_re.reshape(nblk, S5_LANES)
    ai = ab_im.reshape(nblk, S5_LANES)
    pows = [(ar, ai)]
    for _ in range(SUBLANES - 1):
        pr, pi = pows[-1]
        pows.append((pr * ar - pi * ai, pr * ai + pi * ar))
    row = jnp.arange(SUBLANES)[None, :, None]
    planes = []
    for k in (1, 2, 4):
        for comp in (0, 1):
            planes.append(jnp.where(row >= k, pows[k - 1][comp][:, None, :], 0.0))
    planes.append(jnp.stack([pw[0] for pw in pows], axis=1))
    planes.append(jnp.stack([pw[1] for pw in pows], axis=1))
    pw = jnp.stack(planes, axis=1).astype(F32)
    return bw, cw, pw


def _mixer_params(l, a):
    row = lambda v: v.reshape(1, -1).astype(F32)
    p = {}
    p["s5_bw"], p["s5_cw"], p["s5_pw"] = _s5_params(
        a["s5_lam_re"][l], a["s5_lam_im"][l], a["s5_log_dt"][l], a["s5_b_re"][l], a["s5_b_im"][l],
        a["s5_c_re"][l], a["s5_c_im"][l])
    p["s5_d"] = row(a["s5_d"][l])
    p["s5_wglu"] = a["s5_w_glu"][l].astype(BF16)
    p["s5_bglu"] = row(a["s5_b_glu"][l])
    p["gla_wgu"] = jnp.pad(a["gla_w_gate_up"][l].astype(BF16), ((0, GLA_RANK_PAD - GLA_RANK), (0, 0)))
    p["gla_bg"] = row(a["gla_b_gate"][l])
    p["gla_norm"] = row(a["gla_norm"][l])
    p["conv_w"] = a["conv_w"][l].astype(F32)
    p["lru_cw"] = a["lru_conv_w"][l].astype(F32)
    p["lru_cb"] = row(a["lru_conv_b"][l])
    per_blk = LRU_COLS // LRU_BLOCK
    bd = lambda w: _block_diag(w.reshape(-1, per_blk, LRU_BLOCK, LRU_BLOCK)).astype(BF16)
    p["lru_wa"] = bd(a["lru_w_a"][l])
    p["lru_wx"] = bd(a["lru_w_x"][l])
    p["lru_ba"] = row(a["lru_b_a"][l])
    p["lru_bx"] = row(a["lru_b_x"][l])
    p["lru_lam"] = row(a["lru_lam"][l])
    return p


_COL_S5 = 0
_COLS_GLA = (2, 3, 2, 3)
_COLS_CONV = (4, 5, 6)
_COLS_LRU = (7, 8)


def kernel(x, norm_ffn1, ffn1_w_gate, ffn1_w_up, ffn1_w_down, norm_mix, w_in, s5_lam_re, s5_lam_im, s5_log_dt, s5_b_re, s5_b_im, s5_c_re, s5_c_im, s5_d, s5_w_glu, s5_b_glu, gla_w_gate_up, gla_b_gate, gla_norm, conv_w, lru_conv_w, lru_conv_b, lru_w_a, lru_b_a, lru_w_x, lru_b_x, lru_lam, w_br_s5, w_br_gla, w_br_conv, w_br_lru, w_merge, b_merge, w_out, norm_ffn2, ffn2_w_gate, ffn2_w_up, ffn2_w_down, norm_final):
    a = dict(locals())
    bsz, seq, d = x.shape
    depth = norm_ffn1.shape[0]
    glr0 = MIX_W + MIX_W // 2 + MIX_W // 2 + MIX_W
    w_main, w_glr = _permute_cast(w_in, glr0)
    b_merge_r = b_merge.reshape(depth, -1, 1, d)
    w_brs = (w_br_s5, w_br_gla, w_br_conv, w_br_lru)
    w_out_bf = w_out.astype(BF16)

    xf = x.reshape(bsz * seq, d)
    for l in range(depth):
        p = _mixer_params(l, a)
        h = _rmsnorm(xf, norm_ffn1[l], BF16)
        act, wd = _ffn_up(h, ffn1_w_gate, ffn1_w_up, ffn1_w_down, l)
        xf = _mm_res(act, wd, xf, FFN_RES)

        xn = _rmsnorm(xf, norm_mix[l], BF16)
        proj = _matmul(xn, w_main, l, F32, tm=1024, tn=1024)
        glr = _matmul(xn, w_glr, l, F32, tm=1024, tn=GLA_RANK_PAD)
        y_s5 = _s5_mixer(proj, p, bsz, seq, _COL_S5)
        y_gla = _gla_mixer(proj, glr, p, bsz, seq, _COLS_GLA)
        y_conv = _conv_mixer(proj, p["conv_w"], bsz, seq, _COLS_CONV)
        y_lru = _lru_mixer(proj, p, bsz, seq, _COLS_LRU)
        merged = _merge(xn, (y_s5, y_gla, y_conv, y_lru), w_merge, b_merge_r, w_brs, l)
        xf = _matmul_res(merged, w_out_bf, l, xf)

        h = _rmsnorm(xf, norm_ffn2[l], BF16)
        act, wd = _ffn_up(h, ffn2_w_gate, ffn2_w_up, ffn2_w_down, l)
        xf = _mm_res(act, wd, xf, FFN_RES)
    return _rmsnorm(xf, norm_final, F32).reshape(bsz, seq, d)
```

```python
import functools
import math

import jax
import jax.numpy as jnp
from jax import lax
from jax.experimental import pallas as pl
from jax.experimental.pallas import tpu as pltpu

F32 = jnp.float32
BF16 = jnp.bfloat16

EPS = 1e-6
CHUNK = 64
FFN_RES = 0.5
SUBLANES = 8
MIB = 2 ** 20
EPI_ROWS = 16
NORM_UNROLL = 4
MM_RES_CHUNKS = 8
WD_SLAB = 16

S5_GROUP = 16
S5_STATE = 64
S5_COLS = 256
S5_GPC = S5_COLS // S5_GROUP
S5_LANES = S5_GPC * S5_STATE
GLA_HEADS = 4
GLA_RANK = 16
GLA_RANK_PAD = 128
GLA_TAU = 16.0
LRU_BLOCK = 64
LRU_C = 8.0
LRU_COLS = 256
MIX_W = 1024


def _cparams(sem, vmem_mib):
    return pltpu.CompilerParams(dimension_semantics=sem, vmem_limit_bytes=vmem_mib * MIB)


def _dot(a, b):
    return jnp.dot(a, b, preferred_element_type=F32)


def _dot_nt(a, b):
    return lax.dot_general(a, b, (((1,), (1,)), ((), ())), preferred_element_type=F32)


def _dot_tn(a, b):
    return lax.dot_general(a, b, (((0,), (0,)), ((), ())), preferred_element_type=F32)


def _softplus(x):
    return jnp.maximum(x, 0.0) + jnp.log1p(jnp.exp(-jnp.abs(x)))


def _gelu_tanh(x):
    return 0.5 * x * (1.0 + jnp.tanh(math.sqrt(2.0 / math.pi) * (x + 0.044715 * (x * x * x))))


def _rms_scale(x, g):
    ms = jnp.mean(x * x, axis=-1, keepdims=True)
    return x * lax.rsqrt(ms + EPS) * g


def _rmsnorm_kernel(x_ref, g_ref, o_ref):
    g = g_ref[...]

    def rows_body(r, carry):
        rows = pl.ds(pl.multiple_of(r * EPI_ROWS, EPI_ROWS), EPI_ROWS)
        o_ref[rows, :] = _rms_scale(x_ref[rows, :], g).astype(o_ref.dtype)
        return carry

    lax.fori_loop(0, x_ref.shape[0] // EPI_ROWS, rows_body, 0, unroll=NORM_UNROLL)


def _rmsnorm(x, g, out_dtype, tm=512):
    t, d = x.shape
    return pl.pallas_call(
        _rmsnorm_kernel,
        grid=(t // tm,),
        in_specs=[pl.BlockSpec((tm, d), lambda i: (i, 0)),
                  pl.BlockSpec((1, d), lambda i: (0, 0))],
        out_specs=pl.BlockSpec((tm, d), lambda i: (i, 0)),
        out_shape=jax.ShapeDtypeStruct((t, d), out_dtype),
        compiler_params=_cparams(("parallel",), 40),
        name="rmsnorm",
    )(x, g.reshape(1, d))


def _matmul_kernel(x_ref, w_ref, o_ref):
    o_ref[...] = _dot(x_ref[...], w_ref[...]).astype(o_ref.dtype)


def _matmul(x, w, l, out_dtype, tm, tn):
    t, k = x.shape
    n = w.shape[2]
    return pl.pallas_call(
        _matmul_kernel,
        grid=(t // tm, n // tn),
        in_specs=[pl.BlockSpec((tm, k), lambda i, j: (i, 0)),
                  pl.BlockSpec((None, k, tn), lambda i, j: (l, 0, j))],
        out_specs=pl.BlockSpec((tm, tn), lambda i, j: (i, j)),
        out_shape=jax.ShapeDtypeStruct((t, n), out_dtype),
        compiler_params=_cparams(("parallel", "arbitrary"), 48),
        name="proj_matmul",
    )(x, w)


def _matmul_res_kernel(x_ref, w_ref, r_ref, o_ref):
    o_ref[...] = r_ref[...] + _dot(x_ref[...], w_ref[...])


def _matmul_res(x, w, res, tm=1024, tn=1024):
    t, k = x.shape
    n = w.shape[1]
    return pl.pallas_call(
        _matmul_res_kernel,
        grid=(t // tm, n // tn),
        in_specs=[pl.BlockSpec((tm, k), lambda i, j: (i, 0)),
                  pl.BlockSpec((k, tn), lambda i, j: (0, j)),
                  pl.BlockSpec((tm, tn), lambda i, j: (i, j))],
        out_specs=pl.BlockSpec((tm, tn), lambda i, j: (i, j)),
        out_shape=jax.ShapeDtypeStruct((t, n), F32),
        compiler_params=_cparams(("parallel", "arbitrary"), 56),
        name="matmul_res",
    )(x, w, res)


def _ffn_up_kernel(h_ref, wg_ref, wu_ref, wd_ref, o_ref, wdb_ref, wg_bf, wu_bf, *, nj, n_last):
    j = pl.program_id(0)
    i = pl.program_id(1)
    sub = wg_ref.shape[0]
    rows = pl.ds(pl.multiple_of(i * sub, sub), sub)
    wg_bf[j % 2, rows, :] = wg_ref[...].astype(BF16)
    wu_bf[j % 2, rows, :] = wu_ref[...].astype(BF16)
    wdb_ref[...] = wd_ref[...].astype(BF16)
    cur = (j + 1) % 2

    def tile(ncols):
        h = h_ref[...]
        g = _dot(h, wg_bf[cur, :, :ncols])
        u = _dot(h, wu_bf[cur, :, :ncols])
        o_ref[:, :ncols] = (g * jax.nn.sigmoid(g) * u).astype(o_ref.dtype)

    @pl.when((j > 0) & (j < nj))
    def _():
        tile(o_ref.shape[1])

    @pl.when(j == nj)
    def _():
        tile(n_last)


def _ffn_up(h, wg, wu, wd, l, tm=1024, tn=512):
    t, d = h.shape
    f = wg.shape[2]
    nj, ni = pl.cdiv(f, tn), t // tm
    sub = d // ni
    assert sub * ni == d and sub % WD_SLAB == 0
    steps = (nj + 1) * ni
    slab_rows = next(r for r in range(WD_SLAB, f + 1, WD_SLAB) if f % r == 0 and f // r <= steps)
    slabs = f // slab_rows
    slab = lambda j, i: jnp.minimum(j * ni + i, slabs - 1)
    row_tile = lambda j, i: jnp.where(j == 0, 0, i)
    return pl.pallas_call(
        functools.partial(_ffn_up_kernel, nj=nj, n_last=f - (nj - 1) * tn),
        grid=(nj + 1, ni),
        in_specs=[pl.BlockSpec((tm, d), lambda j, i: (row_tile(j, i), 0)),
                  pl.BlockSpec((None, sub, tn), lambda j, i: (l, i, jnp.minimum(j, nj - 1))),
                  pl.BlockSpec((None, sub, tn), lambda j, i: (l, i, jnp.minimum(j, nj - 1))),
                  pl.BlockSpec((None, slab_rows, d), lambda j, i: (l, slab(j, i), 0))],
        out_specs=[pl.BlockSpec((tm, tn), lambda j, i: (row_tile(j, i), jnp.maximum(j - 1, 0))),
                   pl.BlockSpec((slab_rows, d), lambda j, i: (slab(j, i), 0))],
        out_shape=[jax.ShapeDtypeStruct((t, f), BF16),
                   jax.ShapeDtypeStruct((f, d), BF16)],
        scratch_shapes=[pltpu.VMEM((2, d, tn), BF16), pltpu.VMEM((2, d, tn), BF16)],
        compiler_params=_cparams(("arbitrary", "arbitrary"), 56),
        name="ffn_up",
    )(h, wg, wu, wd)


def _mm_res_kernel(a_ref, w_ref, xs_ref, o_ref, acc_ref, xres_ref, res_ref, *, scale, nk, k_last, nchunk, ntiles):
    tile = pl.program_id(0)
    k = pl.program_id(1)
    tk = a_ref.shape[1]
    cr = xs_ref.shape[0]
    kc = jnp.minimum(k, nchunk - 1)
    rows = pl.ds(pl.multiple_of(kc * cr, cr), cr)

    @pl.when(tile > 0)
    def _():
        o_ref[...] = res_ref[rows, :]

    @pl.when(tile < ntiles)
    def _():
        xres_ref[rows, :] = xs_ref[...]

        @pl.when(k == 0)
        def _():
            acc_ref[...] = _dot(a_ref[...], w_ref[...])

        @pl.when((k > 0) & (k < nk - 1))
        def _():
            acc_ref[...] += _dot(a_ref[...], w_ref[...])

        @pl.when(k == nk - 1)
        def _():
            if k_last == tk:
                part = _dot(a_ref[...], w_ref[...])
            else:
                part = _dot(a_ref[:, :k_last], w_ref[:k_last, :])
            res_ref[...] = xres_ref[...] + scale * (acc_ref[...] + part)


def _mm_res(a, w, xres, scale, l=None, tm=1024, tn=2048, tk=1024):
    t, kdim = a.shape
    n = w.shape[-1]
    nk = pl.cdiv(kdim, tk)
    k_last = kdim - (nk - 1) * tk
    ni, nj = t // tm, n // tn
    ntiles = ni * nj
    nchunk = 1 << (min(nk, MM_RES_CHUNKS).bit_length() - 1)
    cr = tm // nchunk
    assert nk >= 2 and cr * nchunk == tm and cr % SUBLANES == 0
    cur = lambda tile: jnp.minimum(tile, ntiles - 1)
    prev = lambda tile: jnp.maximum(tile - 1, 0)
    k_blk = lambda tile, k: jnp.where(tile == ntiles, nk - 1, k)
    chunk = lambda k: jnp.minimum(k, nchunk - 1)
    out_chunk = lambda tile, k: jnp.where(tile == 0, 0, chunk(k))
    if l is None:
        w_spec = pl.BlockSpec((tk, tn), lambda tile, k: (k_blk(tile, k), cur(tile) % nj))
    else:
        w_spec = pl.BlockSpec((None, tk, tn), lambda tile, k: (l, k_blk(tile, k), cur(tile) % nj))
    return pl.pallas_call(
        functools.partial(_mm_res_kernel, scale=scale, nk=nk, k_last=k_last, nchunk=nchunk, ntiles=ntiles),
        grid=(ntiles + 1, nk),
        in_specs=[pl.BlockSpec((tm, tk), lambda tile, k: (cur(tile) // nj, k_blk(tile, k))),
                  w_spec,
                  pl.BlockSpec((cr, tn), lambda tile, k: ((cur(tile) // nj) * nchunk + chunk(k), cur(tile) % nj))],
        out_specs=pl.BlockSpec((cr, tn), lambda tile, k: ((prev(tile) // nj) * nchunk + out_chunk(tile, k),
                                                          prev(tile) % nj)),
        out_shape=jax.ShapeDtypeStruct((t, n), F32),
        scratch_shapes=[pltpu.VMEM((tm, tn), F32), pltpu.VMEM((tm, tn), F32), pltpu.VMEM((tm, tn), F32)],
        compiler_params=_cparams(("arbitrary", "arbitrary"), 60),
        name="mm_res",
    )(a, w, xres)


def _merge_kernel(xn_ref, y0_ref, y1_ref, y2_ref, y3_ref, wm_ref, bm_ref, w0_ref, w1_ref, w2_ref, w3_ref, wo_ref,
                  o_ref, wob_ref, wm_bf, wb_bf):
    wob_ref[...] = wo_ref[...].astype(BF16)
    j = pl.program_id(0)
    i = pl.program_id(1)
    sub_m, sub_b = wm_ref.shape[1], w0_ref.shape[0]
    tn = o_ref.shape[1]
    rows_m = pl.ds(pl.multiple_of(i * sub_m, sub_m), sub_m)
    rows_b = pl.ds(pl.multiple_of(i * sub_b, sub_b), sub_b)
    for n, wb_ref in enumerate((w0_ref, w1_ref, w2_ref, w3_ref)):
        wm_bf[j % 2, rows_m, n * tn:(n + 1) * tn] = wm_ref[n].astype(BF16)
        wb_bf[j % 2, n, rows_b, :] = wb_ref[...].astype(BF16)

    @pl.when(j > 0)
    def _():
        cur = (j + 1) % 2
        pre = _dot(xn_ref[...], wm_bf[cur])
        acc = None
        for n, y_ref in enumerate((y0_ref, y1_ref, y2_ref, y3_ref)):
            gate = jax.nn.sigmoid(pre[:, n * tn:(n + 1) * tn] + bm_ref[n])
            term = gate * _dot(y_ref[...], wb_bf[cur, n])
            acc = term if acc is None else acc + term
        o_ref[...] = acc.astype(o_ref.dtype)


def _merge(xn, ys, wm, bm, wbs, wo, l, tm=512, tn=256):
    t, d = xn.shape
    nb = wm.shape[1]
    w = wbs[0].shape[1]
    nj, ni = d // tn, t // tm
    sub_m, sub_b = d // ni, w // ni
    assert sub_m * ni == d and sub_b * ni == w and sub_b % WD_SLAB == 0
    row_tile = lambda j, i: jnp.where(j == 0, 0, i)
    stage_col = lambda j: jnp.minimum(j, nj - 1)
    out_col = lambda j: jnp.maximum(j - 1, 0)
    y_spec = pl.BlockSpec((tm, w), lambda j, i: (row_tile(j, i), 0))
    wb_spec = pl.BlockSpec((None, sub_b, tn), lambda j, i: (l, i, stage_col(j)))
    ko = wo.shape[1]
    slab_rows = next(r for r in range(WD_SLAB, ko + 1, WD_SLAB) if ko % r == 0 and ko // r <= (nj + 1) * ni)
    slab = lambda j, i: jnp.minimum(j * ni + i, ko // slab_rows - 1)
    return pl.pallas_call(
        _merge_kernel,
        grid=(nj + 1, ni),
        in_specs=[pl.BlockSpec((tm, d), lambda j, i: (row_tile(j, i), 0)),
                  y_spec, y_spec, y_spec, y_spec,
                  pl.BlockSpec((None, nb, sub_m, tn), lambda j, i: (l, 0, i, stage_col(j))),
                  pl.BlockSpec((None, nb, 1, tn), lambda j, i: (l, 0, 0, out_col(j))),
                  wb_spec, wb_spec, wb_spec, wb_spec,
                  pl.BlockSpec((None, slab_rows, d), lambda j, i: (l, slab(j, i), 0))],
        out_specs=[pl.BlockSpec((tm, tn), lambda j, i: (row_tile(j, i), out_col(j))),
                   pl.BlockSpec((slab_rows, d), lambda j, i: (slab(j, i), 0))],
        out_shape=[jax.ShapeDtypeStruct((t, d), BF16),
                   jax.ShapeDtypeStruct((ko, d), BF16)],
        scratch_shapes=[pltpu.VMEM((2, d, nb * tn), BF16), pltpu.VMEM((2, nb, w, tn), BF16)],
        compiler_params=_cparams(("arbitrary", "arbitrary"), 56),
        name="merge",
    )(xn, *ys, wm, bm, *wbs, wo)


def _shift_rows(x, prev8, k):
    xs = pltpu.roll(x, k, 0)
    ps = pltpu.roll(prev8, k, 0)
    row = lax.broadcasted_iota(jnp.int32, prev8.shape, 0)
    head = jnp.where(row < k, ps, xs[:SUBLANES])
    return jnp.concatenate([head, xs[SUBLANES:]], axis=0)


def _conv_kernel(h_ref, b_ref, c_ref, w_ref, o_ref, carry_ref):
    @pl.when(pl.program_id(1) == 0)
    def _():
        carry_ref[...] = jnp.zeros_like(carry_ref)

    z = c_ref[...] * h_ref[...]
    prev = carry_ref[...]
    w = w_ref[...]
    y = w[0:1] * _shift_rows(z, prev, 2) + w[1:2] * _shift_rows(z, prev, 1) + w[2:3] * z
    o_ref[...] = (b_ref[...] * y).astype(o_ref.dtype)
    carry_ref[...] = z[z.shape[0] - SUBLANES:]


def _conv_mixer(proj, w, bsz, seq, cols, ts=512):
    nt = seq // ts
    col = lambda c: pl.BlockSpec((ts, MIX_W), lambda b, t, c=c: (b * nt + t, c))
    return pl.pallas_call(
        _conv_kernel,
        grid=(bsz, nt),
        in_specs=[col(cols[0]), col(cols[1]), col(cols[2]),
                  pl.BlockSpec(w.shape, lambda b, t: (0, 0))],
        out_specs=pl.BlockSpec((ts, MIX_W), lambda b, t: (b * nt + t, 0)),
        out_shape=jax.ShapeDtypeStruct((bsz * seq, MIX_W), BF16),
        scratch_shapes=[pltpu.VMEM((SUBLANES, MIX_W), F32)],
        compiler_params=_cparams(("parallel", "arbitrary"), 32),
        name="conv_mixer",
    )(proj, proj, proj, w)


def _lru_kernel(x_ref, gate_ref, cw_ref, cb_ref, wa_ref, ba_ref, wx_ref, bx_ref, lam_ref, o_ref,
                xprev_ref, hprev_ref, a_s, b_s):
    ts = x_ref.shape[0]

    @pl.when(pl.program_id(1) == 0)
    def _():
        xprev_ref[...] = jnp.zeros_like(xprev_ref)
        hprev_ref[...] = jnp.zeros_like(hprev_ref)

    xb = x_ref[...]
    prev = xprev_ref[...]
    cw = cw_ref[...]
    xc = (cw[0:1] * _shift_rows(xb, prev, 3) + cw[1:2] * _shift_rows(xb, prev, 2)
          + cw[2:3] * _shift_rows(xb, prev, 1) + cw[3:4] * xb) + cb_ref[...]
    xprev_ref[...] = xb[ts - SUBLANES:]

    xcb = xc.astype(BF16)
    ra, rx = [], []
    for c in range(MIX_W // LRU_COLS):
        blk = xcb[:, c * LRU_COLS:(c + 1) * LRU_COLS]
        ra.append(_dot(blk, wa_ref[c]))
        rx.append(_dot(blk, wx_ref[c]))
    r_t = jax.nn.sigmoid(jnp.concatenate(ra, axis=1) + ba_ref[...])
    i_t = jax.nn.sigmoid(jnp.concatenate(rx, axis=1) + bx_ref[...])
    log_a = (-LRU_C) * r_t * _softplus(-lam_ref[...])
    th = jnp.tanh(log_a)
    a_s[...] = jnp.exp(log_a)
    b_s[...] = jnp.sqrt(-2.0 * th / (1.0 - th)) * (i_t * xc)

    row = lax.broadcasted_iota(jnp.int32, (SUBLANES, MIX_W), 0)

    def body(g, hp):
        r0 = pl.multiple_of(g * SUBLANES, SUBLANES)
        a = a_s[pl.ds(r0, SUBLANES), :]
        b = b_s[pl.ds(r0, SUBLANES), :]
        for k in (1, 2, 4):
            a_sh = jnp.where(row >= k, pltpu.roll(a, k, 0), 1.0)
            b_sh = jnp.where(row >= k, pltpu.roll(b, k, 0), 0.0)
            b = a * b_sh + b
            a = a * a_sh
        h = a * hp + b
        b_s[pl.ds(r0, SUBLANES), :] = h
        return jnp.broadcast_to(h[SUBLANES - 1:SUBLANES], (SUBLANES, MIX_W))

    hprev_ref[...] = lax.fori_loop(0, ts // SUBLANES, body, hprev_ref[...], unroll=True)
    o_ref[...] = (b_s[...] * _gelu_tanh(gate_ref[...])).astype(o_ref.dtype)


def _lru_mixer(proj, p, bsz, seq, cols, ts=256):
    nt = seq // ts
    col = lambda c: pl.BlockSpec((ts, MIX_W), lambda b, t, c=c: (b * nt + t, c))
    full = lambda a: pl.BlockSpec(a.shape, lambda b, t, n=a.ndim: (0,) * n)
    params = (p["lru_cw"], p["lru_cb"], p["lru_wa"], p["lru_ba"], p["lru_wx"], p["lru_bx"], p["lru_lam"])
    return pl.pallas_call(
        _lru_kernel,
        grid=(bsz, nt),
        in_specs=[col(cols[0]), col(cols[1])] + [full(a) for a in params],
        out_specs=pl.BlockSpec((ts, MIX_W), lambda b, t: (b * nt + t, 0)),
        out_shape=jax.ShapeDtypeStruct((bsz * seq, MIX_W), BF16),
        scratch_shapes=[pltpu.VMEM((SUBLANES, MIX_W), F32), pltpu.VMEM((SUBLANES, MIX_W), F32),
                        pltpu.VMEM((ts, MIX_W), F32), pltpu.VMEM((ts, MIX_W), F32)],
        compiler_params=_cparams(("parallel", "arbitrary"), 32),
        name="lru_mixer",
    )(proj, proj, *params)


def _s5_kernel(u_ref, bw_ref, cw_ref, pw_ref, d_ref, wglu_ref, bglu_ref, o_ref, s_re, s_im, carry_ref):
    ts = u_ref.shape[0]
    nblk = MIX_W // S5_COLS

    @pl.when(pl.program_id(1) == 0)
    def _():
        carry_ref[...] = jnp.zeros_like(carry_ref)

    u = u_ref[...]
    ub = u.astype(BF16)
    ys = []
    for c in range(nblk):
        bu = _dot(ub[:, c * S5_COLS:(c + 1) * S5_COLS], bw_ref[c])
        s_re[...] = bu[:, :S5_LANES]
        s_im[...] = bu[:, S5_LANES:]

        def body(g, hp, c=c):
            hpr, hpi = hp
            r0 = pl.multiple_of(g * SUBLANES, SUBLANES)
            xr = s_re[pl.ds(r0, SUBLANES), :]
            xi = s_im[pl.ds(r0, SUBLANES), :]
            for lvl, k in enumerate((1, 2, 4)):
                mr = pw_ref[c, 2 * lvl]
                mi = pw_ref[c, 2 * lvl + 1]
                sr = pltpu.roll(xr, k, 0)
                si = pltpu.roll(xi, k, 0)
                xr, xi = xr + (mr * sr - mi * si), xi + (mr * si + mi * sr)
            pr = pw_ref[c, 6]
            pi = pw_ref[c, 7]
            hr = xr + (pr * hpr - pi * hpi)
            hi = xi + (pr * hpi + pi * hpr)
            s_re[pl.ds(r0, SUBLANES), :] = hr
            s_im[pl.ds(r0, SUBLANES), :] = hi
            return (jnp.broadcast_to(hr[SUBLANES - 1:SUBLANES], hr.shape),
                    jnp.broadcast_to(hi[SUBLANES - 1:SUBLANES], hi.shape))

        hpr, hpi = lax.fori_loop(0, ts // SUBLANES, body, (carry_ref[c, 0], carry_ref[c, 1]), unroll=True)
        carry_ref[c, 0] = hpr
        carry_ref[c, 1] = hpi
        hb = jnp.concatenate([s_re[...], s_im[...]], axis=1).astype(BF16)
        ys.append(_dot(hb, cw_ref[c]))
    y = jnp.concatenate(ys, axis=1) + d_ref[...] * u
    y = _gelu_tanh(y)
    z = _dot(y.astype(BF16), wglu_ref[...]) + bglu_ref[...]
    o_ref[...] = (y * jax.nn.sigmoid(z)).astype(o_ref.dtype)


def _s5_mixer(proj, p, bsz, seq, col_idx, ts=256):
    nt = seq // ts
    nblk = MIX_W // S5_COLS
    full = lambda a: pl.BlockSpec(a.shape, lambda b, t, n=a.ndim: (0,) * n)
    params = (p["s5_bw"], p["s5_cw"], p["s5_pw"], p["s5_d"], p["s5_wglu"], p["s5_bglu"])
    return pl.pallas_call(
        _s5_kernel,
        grid=(bsz, nt),
        in_specs=[pl.BlockSpec((ts, MIX_W), lambda b, t: (b * nt + t, col_idx))] + [full(a) for a in params],
        out_specs=pl.BlockSpec((ts, MIX_W), lambda b, t: (b * nt + t, 0)),
        out_shape=jax.ShapeDtypeStruct((bsz * seq, MIX_W), BF16),
        scratch_shapes=[pltpu.VMEM((ts, S5_LANES), F32), pltpu.VMEM((ts, S5_LANES), F32),
                        pltpu.VMEM((nblk, 2, SUBLANES, S5_LANES), F32)],
        compiler_params=_cparams(("parallel", "arbitrary"), 48),
        name="s5_mixer",
    )(proj, *params)


def _gla_kernel(q_ref, k_ref, v_ref, r_ref, glr_ref, wgu_ref, bg_ref, gn_ref, o_ref, state_ref):
    ts = q_ref.shape[0]
    dk = q_ref.shape[1] // GLA_HEADS
    dv = v_ref.shape[1] // GLA_HEADS
    qscale = dk ** -0.5

    @pl.when(pl.program_id(1) == 0)
    def _():
        state_ref[...] = jnp.zeros_like(state_ref)

    ri = lax.broadcasted_iota(jnp.int32, (CHUNK, CHUNK), 0)
    ci = lax.broadcasted_iota(jnp.int32, (CHUNK, CHUNK), 1)
    tri = ri >= ci
    tri_ones = jnp.where(tri, 1.0, 0.0).astype(F32)
    gn = gn_ref[...]

    for n in range(ts // CHUNK):
        rows = slice(n * CHUNK, (n + 1) * CHUNK)
        x = _dot(glr_ref[rows, :].astype(BF16), wgu_ref[...]) + bg_ref[...]
        log_a = -_softplus(-x) / GLA_TAU
        bcum = jnp.dot(tri_ones, log_a, preferred_element_type=F32, precision=lax.Precision.HIGHEST)
        b_last = bcum[CHUNK - 1:CHUNK]
        rel = bcum - bcum[CHUNK // 2 - 1:CHUNK // 2]
        e_pos = jnp.exp(rel)
        e_neg = jnp.exp(-rel)
        e_cum = jnp.exp(bcum)
        e_last = jnp.exp(b_last - bcum)
        decay = jnp.exp(b_last)
        for h in range(GLA_HEADS):
            ks = slice(h * dk, (h + 1) * dk)
            vs = slice(h * dv, (h + 1) * dv)
            qh = q_ref[rows, ks] * qscale
            kh = k_ref[rows, ks]
            vh = v_ref[rows, vs].astype(BF16)
            a_low = _dot_nt((qh * e_pos[:, ks]).astype(BF16), (kh * e_neg[:, ks]).astype(BF16))
            a_up = _dot_nt((qh * e_neg[:, ks]).astype(BF16), (kh * e_pos[:, ks]).astype(BF16))
            attn = jnp.where(tri, a_low, a_up)
            o = _dot(attn.astype(BF16), vh)
            st = state_ref[h]
            o = o + _dot_nt((qh * e_cum[:, ks]).astype(BF16), st.astype(BF16))
            kv_t = _dot_tn(vh, (kh * e_last[:, ks]).astype(BF16))
            state_ref[h] = decay[:, ks] * st + kv_t
            o = o * lax.rsqrt(jnp.mean(o * o, axis=-1, keepdims=True) + EPS) * gn
            rg = r_ref[rows, vs]
            o_ref[rows, vs] = (o * (rg * jax.nn.sigmoid(rg))).astype(o_ref.dtype)


def _gla_mixer(proj, glr, p, bsz, seq, cols, ts=256):
    nt = seq // ts
    kw = MIX_W // 2
    full = lambda a: pl.BlockSpec(a.shape, lambda b, t, n=a.ndim: (0,) * n)
    params = (p["gla_wgu"], p["gla_bg"], p["gla_norm"])
    dv = MIX_W // GLA_HEADS
    dk = kw // GLA_HEADS
    return pl.pallas_call(
        _gla_kernel,
        grid=(bsz, nt),
        in_specs=[pl.BlockSpec((ts, kw), lambda b, t: (b * nt + t, cols[0])),
                  pl.BlockSpec((ts, kw), lambda b, t: (b * nt + t, cols[1])),
                  pl.BlockSpec((ts, MIX_W), lambda b, t: (b * nt + t, cols[2])),
                  pl.BlockSpec((ts, MIX_W), lambda b, t: (b * nt + t, cols[3])),
                  pl.BlockSpec((ts, GLA_RANK_PAD), lambda b, t: (b * nt + t, 0))]
                 + [full(a) for a in params],
        out_specs=pl.BlockSpec((ts, MIX_W), lambda b, t: (b * nt + t, 0)),
        out_shape=jax.ShapeDtypeStruct((bsz * seq, MIX_W), BF16),
        scratch_shapes=[pltpu.VMEM((GLA_HEADS, dv, dk), F32)],
        compiler_params=_cparams(("parallel", "arbitrary"), 32),
        name="gla_mixer",
    )(proj, proj, proj, proj, glr, *params)


def _block_diag(blocks):
    n, g, r, c = blocks.shape
    eye = jnp.eye(g, dtype=blocks.dtype)
    return jnp.einsum('ngrc,gk->ngrkc', blocks, eye).reshape(n, g * r, g * c)


def _s5_params(lam_re, lam_im, log_dt, b_re, b_im, c_re, c_im):
    groups = lam_re.shape[0]
    nblk = groups // S5_GPC
    dt = jnp.exp(log_dt)[:, None]
    mag = jnp.exp(lam_re * dt)
    ab_re, ab_im = mag * jnp.cos(lam_im * dt), mag * jnp.sin(lam_im * dt)
    den = lam_re * lam_re + lam_im * lam_im
    nr, ni = ab_re - 1.0, ab_im
    z_re = (nr * lam_re + ni * lam_im) / den
    z_im = (ni * lam_re - nr * lam_im) / den
    bb_re = z_re[..., None] * b_re - z_im[..., None] * b_im
    bb_im = z_re[..., None] * b_im + z_im[..., None] * b_re
    to_blk = lambda a: _block_diag(jnp.swapaxes(a, 1, 2).reshape(nblk, S5_GPC, S5_GROUP, S5_STATE))
    bw = jnp.concatenate([to_blk(bb_re), to_blk(bb_im)], axis=2).astype(BF16)
    from_blk = lambda a: _block_diag(jnp.swapaxes(a, 1, 2).reshape(nblk, S5_GPC, S5_STATE, S5_GROUP))
    cw = jnp.concatenate([from_blk(c_re), from_blk(-c_im)], axis=1).astype(BF16)
    ar = ab_re.reshape(nblk, S5_LANES)
    ai = ab_im.reshape(nblk, S5_LANES)
    pows = [(ar, ai)]
    for _ in range(SUBLANES - 1):
        pr, pi = pows[-1]
        pows.append((pr * ar - pi * ai, pr * ai + pi * ar))
    row = jnp.arange(SUBLANES)[None, :, None]
    planes = []
    for k in (1, 2, 4):
        for comp in (0, 1):
            planes.append(jnp.where(row >= k, pows[k - 1][comp][:, None, :], 0.0))
    planes.append(jnp.stack([pw[0] for pw in pows], axis=1))
    planes.append(jnp.stack([pw[1] for pw in pows], axis=1))
    pw = jnp.stack(planes, axis=1).astype(F32)
    return bw, cw, pw


def _mixer_params(l, a):
    row = lambda v: v.reshape(1, -1).astype(F32)
    p = {}
    p["s5_bw"], p["s5_cw"], p["s5_pw"] = _s5_params(
        a["s5_lam_re"][l], a["s5_lam_im"][l], a["s5_log_dt"][l], a["s5_b_re"][l], a["s5_b_im"][l],
        a["s5_c_re"][l], a["s5_c_im"][l])
    p["s5_d"] = row(a["s5_d"][l])
    p["s5_wglu"] = a["s5_w_glu"][l].astype(BF16)
    p["s5_bglu"] = row(a["s5_b_glu"][l])
    p["gla_wgu"] = jnp.pad(a["gla_w_gate_up"][l].astype(BF16), ((0, GLA_RANK_PAD - GLA_RANK), (0, 0)))
    p["gla_bg"] = row(a["gla_b_gate"][l])
    p["gla_norm"] = row(a["gla_norm"][l])
    p["conv_w"] = a["conv_w"][l].astype(F32)
    p["lru_cw"] = a["lru_conv_w"][l].astype(F32)
    p["lru_cb"] = row(a["lru_conv_b"][l])
    per_blk = LRU_COLS // LRU_BLOCK
    bd = lambda w: _block_diag(w.reshape(-1, per_blk, LRU_BLOCK, LRU_BLOCK)).astype(BF16)
    p["lru_wa"] = bd(a["lru_w_a"][l])
    p["lru_wx"] = bd(a["lru_w_x"][l])
    p["lru_ba"] = row(a["lru_b_a"][l])
    p["lru_bx"] = row(a["lru_b_x"][l])
    p["lru_lam"] = row(a["lru_lam"][l])
    return p


_COL_S5 = 0
_COLS_GLA = (2, 3, 2, 3)
_COLS_CONV = (4, 5, 6)
_COLS_LRU = (7, 8)


def kernel(x, norm_ffn1, ffn1_w_gate, ffn1_w_up, ffn1_w_down, norm_mix, w_in, s5_lam_re, s5_lam_im, s5_log_dt, s5_b_re, s5_b_im, s5_c_re, s5_c_im, s5_d, s5_w_glu, s5_b_glu, gla_w_gate_up, gla_b_gate, gla_norm, conv_w, lru_conv_w, lru_conv_b, lru_w_a, lru_b_a, lru_w_x, lru_b_x, lru_lam, w_br_s5, w_br_gla, w_br_conv, w_br_lru, w_merge, b_merge, w_out, norm_ffn2, ffn2_w_gate, ffn2_w_up, ffn2_w_down, norm_final):
    a = dict(locals())
    bsz, seq, d = x.shape
    depth = norm_ffn1.shape[0]
    glr0 = MIX_W + MIX_W // 2 + MIX_W // 2 + MIX_W
    w_main = jnp.concatenate([w_in[:, :, :glr0], w_in[:, :, glr0 + GLA_RANK:]], axis=2).astype(BF16)
    w_glr = jnp.pad(w_in[:, :, glr0:glr0 + GLA_RANK].astype(BF16), ((0, 0), (0, 0), (0, GLA_RANK_PAD - GLA_RANK)))
    b_merge_r = b_merge.reshape(depth, -1, 1, d)
    w_brs = (w_br_s5, w_br_gla, w_br_conv, w_br_lru)

    xf = x.reshape(bsz * seq, d)
    for l in range(depth):
        p = _mixer_params(l, a)
        h = _rmsnorm(xf, norm_ffn1[l], BF16)
        act, wd = _ffn_up(h, ffn1_w_gate, ffn1_w_up, ffn1_w_down, l)
        xf = _mm_res(act, wd, xf, FFN_RES)

        xn = _rmsnorm(xf, norm_mix[l], BF16)
        proj = _matmul(xn, w_main, l, F32, tm=1024, tn=1024)
        glr = _matmul(xn, w_glr, l, F32, tm=1024, tn=GLA_RANK_PAD)
        y_s5 = _s5_mixer(proj, p, bsz, seq, _COL_S5)
        y_gla = _gla_mixer(proj, glr, p, bsz, seq, _COLS_GLA)
        y_conv = _conv_mixer(proj, p["conv_w"], bsz, seq, _COLS_CONV)
        y_lru = _lru_mixer(proj, p, bsz, seq, _COLS_LRU)
        merged, wo = _merge(xn, (y_s5, y_gla, y_conv, y_lru), w_merge, b_merge_r, w_brs, w_out, l)
        xf = _matmul_res(merged, wo, xf)

        h = _rmsnorm(xf, norm_ffn2[l], BF16)
        act, wd = _ffn_up(h, ffn2_w_gate, ffn2_w_up, ffn2_w_down, l)
        xf = _mm_res(act, wd, xf, FFN_RES)
    return _rmsnorm(xf, norm_final, F32).reshape(bsz, seq, d)
```

```python
import functools
import math

import jax
import jax.numpy as jnp
from jax import lax
from jax.experimental import pallas as pl
from jax.experimental.pallas import tpu as pltpu

F32 = jnp.float32
BF16 = jnp.bfloat16

EPS = 1e-6
CHUNK = 64
FFN_RES = 0.5
SUBLANES = 8
MIB = 2 ** 20
EPI_ROWS = 16
NORM_UNROLL = 4
MM_RES_CHUNKS = 8
WD_SLAB = 16

S5_GROUP = 16
S5_STATE = 64
S5_COLS = 256
S5_GPC = S5_COLS // S5_GROUP
S5_LANES = S5_GPC * S5_STATE
GLA_HEADS = 4
GLA_RANK = 16
GLA_RANK_PAD = 128
GLA_TAU = 16.0
LRU_BLOCK = 64
LRU_C = 8.0
LRU_COLS = 256
MIX_W = 1024


def _cparams(sem, vmem_mib):
    return pltpu.CompilerParams(dimension_semantics=sem, vmem_limit_bytes=vmem_mib * MIB)


def _dot(a, b):
    return jnp.dot(a, b, preferred_element_type=F32)


def _dot_nt(a, b):
    return lax.dot_general(a, b, (((1,), (1,)), ((), ())), preferred_element_type=F32)


def _dot_tn(a, b):
    return lax.dot_general(a, b, (((0,), (0,)), ((), ())), preferred_element_type=F32)


def _softplus(x):
    return jnp.maximum(x, 0.0) + jnp.log1p(jnp.exp(-jnp.abs(x)))


def _gelu_tanh(x):
    return 0.5 * x * (1.0 + jnp.tanh(math.sqrt(2.0 / math.pi) * (x + 0.044715 * (x * x * x))))


def _rms_scale(x, g):
    ms = jnp.mean(x * x, axis=-1, keepdims=True)
    return x * lax.rsqrt(ms + EPS) * g


def _rmsnorm_kernel(x_ref, g_ref, o_ref):
    g = g_ref[...]

    def rows_body(r, carry):
        rows = pl.ds(pl.multiple_of(r * EPI_ROWS, EPI_ROWS), EPI_ROWS)
        o_ref[rows, :] = _rms_scale(x_ref[rows, :], g).astype(o_ref.dtype)
        return carry

    lax.fori_loop(0, x_ref.shape[0] // EPI_ROWS, rows_body, 0, unroll=NORM_UNROLL)


def _rmsnorm(x, g, out_dtype, tm=512):
    t, d = x.shape
    return pl.pallas_call(
        _rmsnorm_kernel,
        grid=(t // tm,),
        in_specs=[pl.BlockSpec((tm, d), lambda i: (i, 0)),
                  pl.BlockSpec((1, d), lambda i: (0, 0))],
        out_specs=pl.BlockSpec((tm, d), lambda i: (i, 0)),
        out_shape=jax.ShapeDtypeStruct((t, d), out_dtype),
        compiler_params=_cparams(("parallel",), 40),
        name="rmsnorm",
    )(x, g.reshape(1, d))


def _matmul_kernel(x_ref, w_ref, o_ref):
    o_ref[...] = _dot(x_ref[...], w_ref[...]).astype(o_ref.dtype)


def _matmul(x, w, l, out_dtype, tm, tn):
    t, k = x.shape
    n = w.shape[2]
    return pl.pallas_call(
        _matmul_kernel,
        grid=(t // tm, n // tn),
        in_specs=[pl.BlockSpec((tm, k), lambda i, j: (i, 0)),
                  pl.BlockSpec((None, k, tn), lambda i, j: (l, 0, j))],
        out_specs=pl.BlockSpec((tm, tn), lambda i, j: (i, j)),
        out_shape=jax.ShapeDtypeStruct((t, n), out_dtype),
        compiler_params=_cparams(("parallel", "arbitrary"), 48),
        name="proj_matmul",
    )(x, w)


def _matmul_res_kernel(x_ref, w_ref, r_ref, o_ref):
    o_ref[...] = r_ref[...] + _dot(x_ref[...], w_ref[...])


def _matmul_res(x, w, res, tm=1024, tn=1024):
    t, k = x.shape
    n = w.shape[1]
    return pl.pallas_call(
        _matmul_res_kernel,
        grid=(t // tm, n // tn),
        in_specs=[pl.BlockSpec((tm, k), lambda i, j: (i, 0)),
                  pl.BlockSpec((k, tn), lambda i, j: (0, j)),
                  pl.BlockSpec((tm, tn), lambda i, j: (i, j))],
        out_specs=pl.BlockSpec((tm, tn), lambda i, j: (i, j)),
        out_shape=jax.ShapeDtypeStruct((t, n), F32),
        compiler_params=_cparams(("parallel", "arbitrary"), 56),
        name="matmul_res",
    )(x, w, res)


def _ffn_up_kernel(h_ref, wg_ref, wu_ref, wd_ref, o_ref, wdb_ref, wg_bf, wu_bf, *, nj, n_last):
    j = pl.program_id(0)
    i = pl.program_id(1)
    sub = wg_ref.shape[0]
    rows = pl.ds(pl.multiple_of(i * sub, sub), sub)
    wg_bf[j % 2, rows, :] = wg_ref[...].astype(BF16)
    wu_bf[j % 2, rows, :] = wu_ref[...].astype(BF16)
    wdb_ref[...] = wd_ref[...].astype(BF16)
    cur = (j + 1) % 2

    def tile(ncols):
        h = h_ref[...]
        g = _dot(h, wg_bf[cur, :, :ncols])
        u = _dot(h, wu_bf[cur, :, :ncols])
        o_ref[:, :ncols] = (g * jax.nn.sigmoid(g) * u).astype(o_ref.dtype)

    @pl.when((j > 0) & (j < nj))
    def _():
        tile(o_ref.shape[1])

    @pl.when(j == nj)
    def _():
        tile(n_last)


def _ffn_up(h, wg, wu, wd, l, tm=1024, tn=512):
    t, d = h.shape
    f = wg.shape[2]
    nj, ni = pl.cdiv(f, tn), t // tm
    sub = d // ni
    assert sub * ni == d and sub % WD_SLAB == 0
    steps = (nj + 1) * ni
    slab_rows = next(r for r in range(WD_SLAB, f + 1, WD_SLAB) if f % r == 0 and f // r <= steps)
    slabs = f // slab_rows
    slab = lambda j, i: jnp.minimum(j * ni + i, slabs - 1)
    row_tile = lambda j, i: jnp.where(j == 0, 0, i)
    return pl.pallas_call(
        functools.partial(_ffn_up_kernel, nj=nj, n_last=f - (nj - 1) * tn),
        grid=(nj + 1, ni),
        in_specs=[pl.BlockSpec((tm, d), lambda j, i: (row_tile(j, i), 0)),
                  pl.BlockSpec((None, sub, tn), lambda j, i: (l, i, jnp.minimum(j, nj - 1))),
                  pl.BlockSpec((None, sub, tn), lambda j, i: (l, i, jnp.minimum(j, nj - 1))),
                  pl.BlockSpec((None, slab_rows, d), lambda j, i: (l, slab(j, i), 0))],
        out_specs=[pl.BlockSpec((tm, tn), lambda j, i: (row_tile(j, i), jnp.maximum(j - 1, 0))),
                   pl.BlockSpec((slab_rows, d), lambda j, i: (slab(j, i), 0))],
        out_shape=[jax.ShapeDtypeStruct((t, f), BF16),
                   jax.ShapeDtypeStruct((f, d), BF16)],
        scratch_shapes=[pltpu.VMEM((2, d, tn), BF16), pltpu.VMEM((2, d, tn), BF16)],
        compiler_params=_cparams(("arbitrary", "arbitrary"), 56),
        name="ffn_up",
    )(h, wg, wu, wd)


def _mm_res_kernel(a_ref, w_ref, xs_ref, o_ref, acc_ref, xres_ref, res_ref, *, scale, nk, k_last, nchunk, ntiles):
    tile = pl.program_id(0)
    k = pl.program_id(1)
    tk = a_ref.shape[1]
    cr = xs_ref.shape[0]
    kc = jnp.minimum(k, nchunk - 1)
    rows = pl.ds(pl.multiple_of(kc * cr, cr), cr)

    @pl.when(tile > 0)
    def _():
        o_ref[...] = res_ref[rows, :]

    @pl.when(tile < ntiles)
    def _():
        xres_ref[rows, :] = xs_ref[...]

        @pl.when(k == 0)
        def _():
            acc_ref[...] = _dot(a_ref[...], w_ref[...])

        @pl.when((k > 0) & (k < nk - 1))
        def _():
            acc_ref[...] += _dot(a_ref[...], w_ref[...])

        @pl.when(k == nk - 1)
        def _():
            if k_last == tk:
                part = _dot(a_ref[...], w_ref[...])
            else:
                part = _dot(a_ref[:, :k_last], w_ref[:k_last, :])
            res_ref[...] = xres_ref[...] + scale * (acc_ref[...] + part)


def _mm_res(a, w, xres, scale, l=None, tm=1024, tn=2048, tk=1024):
    t, kdim = a.shape
    n = w.shape[-1]
    nk = pl.cdiv(kdim, tk)
    k_last = kdim - (nk - 1) * tk
    ni, nj = t // tm, n // tn
    ntiles = ni * nj
    nchunk = 1 << (min(nk, MM_RES_CHUNKS).bit_length() - 1)
    cr = tm // nchunk
    assert nk >= 2 and cr * nchunk == tm and cr % SUBLANES == 0
    cur = lambda tile: jnp.minimum(tile, ntiles - 1)
    prev = lambda tile: jnp.maximum(tile - 1, 0)
    k_blk = lambda tile, k: jnp.where(tile == ntiles, nk - 1, k)
    chunk = lambda k: jnp.minimum(k, nchunk - 1)
    out_chunk = lambda tile, k: jnp.where(tile == 0, 0, chunk(k))
    if l is None:
        w_spec = pl.BlockSpec((tk, tn), lambda tile, k: (k_blk(tile, k), cur(tile) % nj))
    else:
        w_spec = pl.BlockSpec((None, tk, tn), lambda tile, k: (l, k_blk(tile, k), cur(tile) % nj))
    return pl.pallas_call(
        functools.partial(_mm_res_kernel, scale=scale, nk=nk, k_last=k_last, nchunk=nchunk, ntiles=ntiles),
        grid=(ntiles + 1, nk),
        in_specs=[pl.BlockSpec((tm, tk), lambda tile, k: (cur(tile) // nj, k_blk(tile, k))),
                  w_spec,
                  pl.BlockSpec((cr, tn), lambda tile, k: ((cur(tile) // nj) * nchunk + chunk(k), cur(tile) % nj))],
        out_specs=pl.BlockSpec((cr, tn), lambda tile, k: ((prev(tile) // nj) * nchunk + out_chunk(tile, k),
                                                          prev(tile) % nj)),
        out_shape=jax.ShapeDtypeStruct((t, n), F32),
        scratch_shapes=[pltpu.VMEM((tm, tn), F32), pltpu.VMEM((tm, tn), F32), pltpu.VMEM((tm, tn), F32)],
        compiler_params=_cparams(("arbitrary", "arbitrary"), 60),
        name="mm_res",
    )(a, w, xres)


def _merge_kernel(xn_ref, y0_ref, y1_ref, y2_ref, y3_ref, wm_ref, bm_ref, w0_ref, w1_ref, w2_ref, w3_ref, wo_ref,
                  o_ref, wob_ref, wm_bf, wb_bf):
    wob_ref[...] = wo_ref[...].astype(BF16)
    j = pl.program_id(0)
    i = pl.program_id(1)
    sub_m, sub_b = wm_ref.shape[1], w0_ref.shape[0]
    tn = o_ref.shape[1]
    rows_m = pl.ds(pl.multiple_of(i * sub_m, sub_m), sub_m)
    rows_b = pl.ds(pl.multiple_of(i * sub_b, sub_b), sub_b)
    for n, wb_ref in enumerate((w0_ref, w1_ref, w2_ref, w3_ref)):
        wm_bf[j % 2, rows_m, n * tn:(n + 1) * tn] = wm_ref[n].astype(BF16)
        wb_bf[j % 2, n, rows_b, :] = wb_ref[...].astype(BF16)

    @pl.when(j > 0)
    def _():
        cur = (j + 1) % 2
        pre = _dot(xn_ref[...], wm_bf[cur])
        acc = None
        for n, y_ref in enumerate((y0_ref, y1_ref, y2_ref, y3_ref)):
            gate = jax.nn.sigmoid(pre[:, n * tn:(n + 1) * tn] + bm_ref[n])
            term = gate * _dot(y_ref[...], wb_bf[cur, n])
            acc = term if acc is None else acc + term
        o_ref[...] = acc.astype(o_ref.dtype)


def _merge(xn, ys, wm, bm, wbs, wo, l, tm=512, tn=256):
    t, d = xn.shape
    nb = wm.shape[1]
    w = wbs[0].shape[1]
    nj, ni = d // tn, t // tm
    sub_m, sub_b = d // ni, w // ni
    assert sub_m * ni == d and sub_b * ni == w and sub_b % WD_SLAB == 0
    row_tile = lambda j, i: jnp.where(j == 0, 0, i)
    stage_col = lambda j: jnp.minimum(j, nj - 1)
    out_col = lambda j: jnp.maximum(j - 1, 0)
    y_spec = pl.BlockSpec((tm, w), lambda j, i: (row_tile(j, i), 0))
    wb_spec = pl.BlockSpec((None, sub_b, tn), lambda j, i: (l, i, stage_col(j)))
    ko = wo.shape[1]
    slab_rows = next(r for r in range(WD_SLAB, ko + 1, WD_SLAB) if ko % r == 0 and ko // r <= (nj + 1) * ni)
    slab = lambda j, i: jnp.minimum(j * ni + i, ko // slab_rows - 1)
    return pl.pallas_call(
        _merge_kernel,
        grid=(nj + 1, ni),
        in_specs=[pl.BlockSpec((tm, d), lambda j, i: (row_tile(j, i), 0)),
                  y_spec, y_spec, y_spec, y_spec,
                  pl.BlockSpec((None, nb, sub_m, tn), lambda j, i: (l, 0, i, stage_col(j))),
                  pl.BlockSpec((None, nb, 1, tn), lambda j, i: (l, 0, 0, out_col(j))),
                  wb_spec, wb_spec, wb_spec, wb_spec,
                  pl.BlockSpec((None, slab_rows, d), lambda j, i: (l, slab(j, i), 0))],
        out_specs=[pl.BlockSpec((tm, tn), lambda j, i: (row_tile(j, i), out_col(j))),
                   pl.BlockSpec((slab_rows, d), lambda j, i: (slab(j, i), 0))],
        out_shape=[jax.ShapeDtypeStruct((t, d), BF16),
                   jax.ShapeDtypeStruct((ko, d), BF16)],
        scratch_shapes=[pltpu.VMEM((2, d, nb * tn), BF16), pltpu.VMEM((2, nb, w, tn), BF16)],
        compiler_params=_cparams(("arbitrary", "arbitrary"), 56),
        name="merge",
    )(xn, *ys, wm, bm, *wbs, wo)


def _shift_rows(x, prev8, k):
    xs = pltpu.roll(x, k, 0)
    ps = pltpu.roll(prev8, k, 0)
    row = lax.broadcasted_iota(jnp.int32, prev8.shape, 0)
    head = jnp.where(row < k, ps, xs[:SUBLANES])
    return jnp.concatenate([head, xs[SUBLANES:]], axis=0)


def _conv_body(h_ref, b_ref, c_ref, w_ref, o_ref, carry_ref):
    z = c_ref[...] * h_ref[...]
    prev = carry_ref[...]
    w = w_ref[...]
    y = w[0:1] * _shift_rows(z, prev, 2) + w[1:2] * _shift_rows(z, prev, 1) + w[2:3] * z
    o_ref[...] = (b_ref[...] * y).astype(o_ref.dtype)
    carry_ref[...] = z[z.shape[0] - SUBLANES:]


def _lru_body(x_ref, gate_ref, cw_ref, cb_ref, wa_ref, ba_ref, wx_ref, bx_ref, lam_ref, o_ref,
              xprev_ref, hprev_ref, a_s, b_s):
    ts = x_ref.shape[0]
    xb = x_ref[...]
    prev = xprev_ref[...]
    cw = cw_ref[...]
    xc = (cw[0:1] * _shift_rows(xb, prev, 3) + cw[1:2] * _shift_rows(xb, prev, 2)
          + cw[2:3] * _shift_rows(xb, prev, 1) + cw[3:4] * xb) + cb_ref[...]
    xprev_ref[...] = xb[ts - SUBLANES:]

    xcb = xc.astype(BF16)
    ra, rx = [], []
    for c in range(MIX_W // LRU_COLS):
        blk = xcb[:, c * LRU_COLS:(c + 1) * LRU_COLS]
        ra.append(_dot(blk, wa_ref[c]))
        rx.append(_dot(blk, wx_ref[c]))
    r_t = jax.nn.sigmoid(jnp.concatenate(ra, axis=1) + ba_ref[...])
    i_t = jax.nn.sigmoid(jnp.concatenate(rx, axis=1) + bx_ref[...])
    log_a = (-LRU_C) * r_t * _softplus(-lam_ref[...])
    th = jnp.tanh(log_a)
    a_s[...] = jnp.exp(log_a)
    b_s[...] = jnp.sqrt(-2.0 * th / (1.0 - th)) * (i_t * xc)

    row = lax.broadcasted_iota(jnp.int32, (SUBLANES, MIX_W), 0)

    def body(g, hp):
        r0 = pl.multiple_of(g * SUBLANES, SUBLANES)
        a = a_s[pl.ds(r0, SUBLANES), :]
        b = b_s[pl.ds(r0, SUBLANES), :]
        for k in (1, 2, 4):
            a_sh = jnp.where(row >= k, pltpu.roll(a, k, 0), 1.0)
            b_sh = jnp.where(row >= k, pltpu.roll(b, k, 0), 0.0)
            b = a * b_sh + b
            a = a * a_sh
        h = a * hp + b
        b_s[pl.ds(r0, SUBLANES), :] = h
        return jnp.broadcast_to(h[SUBLANES - 1:SUBLANES], (SUBLANES, MIX_W))

    hprev_ref[...] = lax.fori_loop(0, ts // SUBLANES, body, hprev_ref[...], unroll=True)
    o_ref[...] = (b_s[...] * _gelu_tanh(gate_ref[...])).astype(o_ref.dtype)


def _s5_kernel(u_ref, bw_ref, cw_ref, pw_ref, d_ref, wglu_ref, bglu_ref, o_ref, s_re, s_im, carry_ref):
    ts = u_ref.shape[0]
    nblk = MIX_W // S5_COLS

    @pl.when(pl.program_id(1) == 0)
    def _():
        carry_ref[...] = jnp.zeros_like(carry_ref)

    u = u_ref[...]
    ub = u.astype(BF16)
    ys = []
    for c in range(nblk):
        bu = _dot(ub[:, c * S5_COLS:(c + 1) * S5_COLS], bw_ref[c])
        s_re[...] = bu[:, :S5_LANES]
        s_im[...] = bu[:, S5_LANES:]

        def body(g, hp, c=c):
            hpr, hpi = hp
            r0 = pl.multiple_of(g * SUBLANES, SUBLANES)
            xr = s_re[pl.ds(r0, SUBLANES), :]
            xi = s_im[pl.ds(r0, SUBLANES), :]
            for lvl, k in enumerate((1, 2, 4)):
                mr = pw_ref[c, 2 * lvl]
                mi = pw_ref[c, 2 * lvl + 1]
                sr = pltpu.roll(xr, k, 0)
                si = pltpu.roll(xi, k, 0)
                xr, xi = xr + (mr * sr - mi * si), xi + (mr * si + mi * sr)
            pr = pw_ref[c, 6]
            pi = pw_ref[c, 7]
            hr = xr + (pr * hpr - pi * hpi)
            hi = xi + (pr * hpi + pi * hpr)
            s_re[pl.ds(r0, SUBLANES), :] = hr
            s_im[pl.ds(r0, SUBLANES), :] = hi
            return (jnp.broadcast_to(hr[SUBLANES - 1:SUBLANES], hr.shape),
                    jnp.broadcast_to(hi[SUBLANES - 1:SUBLANES], hi.shape))

        hpr, hpi = lax.fori_loop(0, ts // SUBLANES, body, (carry_ref[c, 0], carry_ref[c, 1]), unroll=True)
        carry_ref[c, 0] = hpr
        carry_ref[c, 1] = hpi
        hb = jnp.concatenate([s_re[...], s_im[...]], axis=1).astype(BF16)
        ys.append(_dot(hb, cw_ref[c]))
    y = jnp.concatenate(ys, axis=1) + d_ref[...] * u
    y = _gelu_tanh(y)
    z = _dot(y.astype(BF16), wglu_ref[...]) + bglu_ref[...]
    o_ref[...] = (y * jax.nn.sigmoid(z)).astype(o_ref.dtype)


def _s5_mixer(proj, p, bsz, seq, col_idx, ts=256):
    nt = seq // ts
    nblk = MIX_W // S5_COLS
    full = lambda a: pl.BlockSpec(a.shape, lambda b, t, n=a.ndim: (0,) * n)
    params = (p["s5_bw"], p["s5_cw"], p["s5_pw"], p["s5_d"], p["s5_wglu"], p["s5_bglu"])
    return pl.pallas_call(
        _s5_kernel,
        grid=(bsz, nt),
        in_specs=[pl.BlockSpec((ts, MIX_W), lambda b, t: (b * nt + t, col_idx))] + [full(a) for a in params],
        out_specs=pl.BlockSpec((ts, MIX_W), lambda b, t: (b * nt + t, 0)),
        out_shape=jax.ShapeDtypeStruct((bsz * seq, MIX_W), BF16),
        scratch_shapes=[pltpu.VMEM((ts, S5_LANES), F32), pltpu.VMEM((ts, S5_LANES), F32),
                        pltpu.VMEM((nblk, 2, SUBLANES, S5_LANES), F32)],
        compiler_params=_cparams(("parallel", "arbitrary"), 48),
        name="s5_mixer",
    )(proj, *params)


def _gla_body(q_ref, k_ref, v_ref, r_ref, glr_ref, wgu_ref, bg_ref, gn_ref, o_ref, state_ref):
    ts = q_ref.shape[0]
    dk = q_ref.shape[1] // GLA_HEADS
    dv = v_ref.shape[1] // GLA_HEADS
    qscale = dk ** -0.5

    ri = lax.broadcasted_iota(jnp.int32, (CHUNK, CHUNK), 0)
    ci = lax.broadcasted_iota(jnp.int32, (CHUNK, CHUNK), 1)
    tri = ri >= ci
    tri_ones = jnp.where(tri, 1.0, 0.0).astype(F32)
    gn = gn_ref[...]

    for n in range(ts // CHUNK):
        rows = slice(n * CHUNK, (n + 1) * CHUNK)
        x = _dot(glr_ref[rows, :].astype(BF16), wgu_ref[...]) + bg_ref[...]
        log_a = -_softplus(-x) / GLA_TAU
        bcum = jnp.dot(tri_ones, log_a, preferred_element_type=F32, precision=lax.Precision.HIGHEST)
        b_last = bcum[CHUNK - 1:CHUNK]
        rel = bcum - bcum[CHUNK // 2 - 1:CHUNK // 2]
        e_pos = jnp.exp(rel)
        e_neg = jnp.exp(-rel)
        e_cum = jnp.exp(bcum)
        e_last = jnp.exp(b_last - bcum)
        decay = jnp.exp(b_last)
        for h in range(GLA_HEADS):
            ks = slice(h * dk, (h + 1) * dk)
            vs = slice(h * dv, (h + 1) * dv)
            qh = q_ref[rows, ks] * qscale
            kh = k_ref[rows, ks]
            vh = v_ref[rows, vs].astype(BF16)
            a_low = _dot_nt((qh * e_pos[:, ks]).astype(BF16), (kh * e_neg[:, ks]).astype(BF16))
            a_up = _dot_nt((qh * e_neg[:, ks]).astype(BF16), (kh * e_pos[:, ks]).astype(BF16))
            attn = jnp.where(tri, a_low, a_up)
            o = _dot(attn.astype(BF16), vh)
            st = state_ref[h]
            o = o + _dot_nt((qh * e_cum[:, ks]).astype(BF16), st.astype(BF16))
            kv_t = _dot_tn(vh, (kh * e_last[:, ks]).astype(BF16))
            state_ref[h] = decay[:, ks] * st + kv_t
            o = o * lax.rsqrt(jnp.mean(o * o, axis=-1, keepdims=True) + EPS) * gn
            rg = r_ref[rows, vs]
            o_ref[rows, vs] = (o * (rg * jax.nn.sigmoid(rg))).astype(o_ref.dtype)


def _mix3_kernel(*refs):
    gla_in, lru_in, conv_in = refs[:8], refs[8:17], refs[17:21]
    o_gla, o_lru, o_conv = refs[21:24]
    state, xprev, hprev, a_s, b_s, carry = refs[24:]

    @pl.when(pl.program_id(1) == 0)
    def _():
        for ref in (state, xprev, hprev, carry):
            ref[...] = jnp.zeros_like(ref)

    _gla_body(*gla_in, o_gla, state)
    _lru_body(*lru_in, o_lru, xprev, hprev, a_s, b_s)
    _conv_body(*conv_in, o_conv, carry)


def _mix3(proj, glr, p, bsz, seq, ts=256):
    nt = seq // ts
    kw = MIX_W // 2
    dv, dk = MIX_W // GLA_HEADS, kw // GLA_HEADS
    rows = lambda b, t: b * nt + t
    col = lambda c, w=MIX_W: pl.BlockSpec((ts, w), lambda b, t, c=c: (rows(b, t), c))
    full = lambda a: pl.BlockSpec(a.shape, lambda b, t, n=a.ndim: (0,) * n)
    gla_p = (p["gla_wgu"], p["gla_bg"], p["gla_norm"])
    lru_p = (p["lru_cw"], p["lru_cb"], p["lru_wa"], p["lru_ba"], p["lru_wx"], p["lru_bx"], p["lru_lam"])
    conv_p = (p["conv_w"],)
    in_specs = ([col(_COLS_GLA[0], kw), col(_COLS_GLA[1], kw), col(_COLS_GLA[2]), col(_COLS_GLA[3]),
                 pl.BlockSpec((ts, GLA_RANK_PAD), lambda b, t: (rows(b, t), 0))] + [full(a) for a in gla_p]
                + [col(_COLS_LRU[0]), col(_COLS_LRU[1])] + [full(a) for a in lru_p]
                + [col(c) for c in _COLS_CONV] + [full(a) for a in conv_p])
    out = jax.ShapeDtypeStruct((bsz * seq, MIX_W), BF16)
    out_spec = pl.BlockSpec((ts, MIX_W), lambda b, t: (rows(b, t), 0))
    y_gla, y_lru, y_conv = pl.pallas_call(
        _mix3_kernel,
        grid=(bsz, nt),
        in_specs=in_specs,
        out_specs=[out_spec] * 3,
        out_shape=[out] * 3,
        scratch_shapes=[pltpu.VMEM((GLA_HEADS, dv, dk), F32),
                        pltpu.VMEM((SUBLANES, MIX_W), F32), pltpu.VMEM((SUBLANES, MIX_W), F32),
                        pltpu.VMEM((ts, MIX_W), F32), pltpu.VMEM((ts, MIX_W), F32),
                        pltpu.VMEM((SUBLANES, MIX_W), F32)],
        compiler_params=_cparams(("parallel", "arbitrary"), 48),
        name="mix3",
    )(proj, proj, proj, proj, glr, *gla_p, proj, proj, *lru_p, proj, proj, proj, *conv_p)
    return y_gla, y_conv, y_lru


def _block_diag(blocks):
    n, g, r, c = blocks.shape
    eye = jnp.eye(g, dtype=blocks.dtype)
    return jnp.einsum('ngrc,gk->ngrkc', blocks, eye).reshape(n, g * r, g * c)


def _s5_params(lam_re, lam_im, log_dt, b_re, b_im, c_re, c_im):
    groups = lam_re.shape[0]
    nblk = groups // S5_GPC
    dt = jnp.exp(log_dt)[:, None]
    mag = jnp.exp(lam_re * dt)
    ab_re, ab_im = mag * jnp.cos(lam_im * dt), mag * jnp.sin(lam_im * dt)
    den = lam_re * lam_re + lam_im * lam_im
    nr, ni = ab_re - 1.0, ab_im
    z_re = (nr * lam_re + ni * lam_im) / den
    z_im = (ni * lam_re - nr * lam_im) / den
    bb_re = z_re[..., None] * b_re - z_im[..., None] * b_im
    bb_im = z_re[..., None] * b_im + z_im[..., None] * b_re
    to_blk = lambda a: _block_diag(jnp.swapaxes(a, 1, 2).reshape(nblk, S5_GPC, S5_GROUP, S5_STATE))
    bw = jnp.concatenate([to_blk(bb_re), to_blk(bb_im)], axis=2).astype(BF16)
    from_blk = lambda a: _block_diag(jnp.swapaxes(a, 1, 2).reshape(nblk, S5_GPC, S5_STATE, S5_GROUP))
    cw = jnp.concatenate([from_blk(c_re), from_blk(-c_im)], axis=1).astype(BF16)
    ar = ab_re.reshape(nblk, S5_LANES)
    ai = ab_im.reshape(nblk, S5_LANES)
    pows = [(ar, ai)]
    for _ in range(SUBLANES - 1):
        pr, pi = pows[-1]
        pows.append((pr * ar - pi * ai, pr * ai + pi * ar))
    row = jnp.arange(SUBLANES)[None, :, None]
    planes = []
    for k in (1, 2, 4):
        for comp in (0, 1):
            planes.append(jnp.where(row >= k, pows[k - 1][comp][:, None, :], 0.0))
    planes.append(jnp.stack([pw[0] for pw in pows], axis=1))
    planes.append(jnp.stack([pw[1] for pw in pows], axis=1))
    pw = jnp.stack(planes, axis=1).astype(F32)
    return bw, cw, pw


def _mixer_params(l, a):
    row = lambda v: v.reshape(1, -1).astype(F32)
    p = {}
    p["s5_bw"], p["s5_cw"], p["s5_pw"] = _s5_params(
        a["s5_lam_re"][l], a["s5_lam_im"][l], a["s5_log_dt"][l], a["s5_b_re"][l], a["s5_b_im"][l],
        a["s5_c_re"][l], a["s5_c_im"][l])
    p["s5_d"] = row(a["s5_d"][l])
    p["s5_wglu"] = a["s5_w_glu"][l].astype(BF16)
    p["s5_bglu"] = row(a["s5_b_glu"][l])
    p["gla_wgu"] = jnp.pad(a["gla_w_gate_up"][l].astype(BF16), ((0, GLA_RANK_PAD - GLA_RANK), (0, 0)))
    p["gla_bg"] = row(a["gla_b_gate"][l])
    p["gla_norm"] = row(a["gla_norm"][l])
    p["conv_w"] = a["conv_w"][l].astype(F32)
    p["lru_cw"] = a["lru_conv_w"][l].astype(F32)
    p["lru_cb"] = row(a["lru_conv_b"][l])
    per_blk = LRU_COLS // LRU_BLOCK
    bd = lambda w: _block_diag(w.reshape(-1, per_blk, LRU_BLOCK, LRU_BLOCK)).astype(BF16)
    p["lru_wa"] = bd(a["lru_w_a"][l])
    p["lru_wx"] = bd(a["lru_w_x"][l])
    p["lru_ba"] = row(a["lru_b_a"][l])
    p["lru_bx"] = row(a["lru_b_x"][l])
    p["lru_lam"] = row(a["lru_lam"][l])
    return p


_COL_S5 = 0
_COLS_GLA = (2, 3, 2, 3)
_COLS_CONV = (4, 5, 6)
_COLS_LRU = (7, 8)


def kernel(x, norm_ffn1, ffn1_w_gate, ffn1_w_up, ffn1_w_down, norm_mix, w_in, s5_lam_re, s5_lam_im, s5_log_dt, s5_b_re, s5_b_im, s5_c_re, s5_c_im, s5_d, s5_w_glu, s5_b_glu, gla_w_gate_up, gla_b_gate, gla_norm, conv_w, lru_conv_w, lru_conv_b, lru_w_a, lru_b_a, lru_w_x, lru_b_x, lru_lam, w_br_s5, w_br_gla, w_br_conv, w_br_lru, w_merge, b_merge, w_out, norm_ffn2, ffn2_w_gate, ffn2_w_up, ffn2_w_down, norm_final):
    a = dict(locals())
    bsz, seq, d = x.shape
    depth = norm_ffn1.shape[0]
    glr0 = MIX_W + MIX_W // 2 + MIX_W // 2 + MIX_W
    w_main = jnp.concatenate([w_in[:, :, :glr0], w_in[:, :, glr0 + GLA_RANK:]], axis=2).astype(BF16)
    w_glr = jnp.pad(w_in[:, :, glr0:glr0 + GLA_RANK].astype(BF16), ((0, 0), (0, 0), (0, GLA_RANK_PAD - GLA_RANK)))
    b_merge_r = b_merge.reshape(depth, -1, 1, d)
    w_brs = (w_br_s5, w_br_gla, w_br_conv, w_br_lru)

    xf = x.reshape(bsz * seq, d)
    for l in range(depth):
        p = _mixer_params(l, a)
        h = _rmsnorm(xf, norm_ffn1[l], BF16)
        act, wd = _ffn_up(h, ffn1_w_gate, ffn1_w_up, ffn1_w_down, l)
        xf = _mm_res(act, wd, xf, FFN_RES)

        xn = _rmsnorm(xf, norm_mix[l], BF16)
        proj = _matmul(xn, w_main, l, F32, tm=1024, tn=1024)
        glr = _matmul(xn, w_glr, l, F32, tm=1024, tn=GLA_RANK_PAD)
        y_s5 = _s5_mixer(proj, p, bsz, seq, _COL_S5)
        y_gla, y_conv, y_lru = _mix3(proj, glr, p, bsz, seq)
        merged, wo = _merge(xn, (y_s5, y_gla, y_conv, y_lru), w_merge, b_merge_r, w_brs, w_out, l)
        xf = _matmul_res(merged, wo, xf)

        h = _rmsnorm(xf, norm_ffn2[l], BF16)
        act, wd = _ffn_up(h, ffn2_w_gate, ffn2_w_up, ffn2_w_down, l)
        xf = _mm_res(act, wd, xf, FFN_RES)
    return _rmsnorm(xf, norm_final, F32).reshape(bsz, seq, d)
```
